```python
import jax
import jax.numpy as jnp
from jax import lax
import numpy as np

D_MODEL = 2048
BATCH = 2
SEQ = 16384
DEPTH = 1
DEC_BATCH = 8
DEC_SEQ = 16
PAST_LEN = 2048

CHUNK = 64
N_Q_HEADS = 16
N_KV_HEADS = 2
HEAD_DIM = 64
Q_PER_KV = N_Q_HEADS // N_KV_HEADS
ATTN_WIDTH = N_Q_HEADS * HEAD_DIM
KV_WIDTH = N_KV_HEADS * HEAD_DIM
WINDOW = 128
WINDOW_CHUNKS = WINDOW // CHUNK
ROT_DIM = HEAD_DIM // 4
ROPE_THETA = 500000.0
ATTN_SCALE = HEAD_DIM ** -0.5
NEG_INF = -1e30
POOL_WINDOWS = (2, 4, 8, 16)
N_POOL_GROUPS = len(POOL_WINDOWS)
POOL_WIDTH = D_MODEL // 2
POOL_GROUP = POOL_WIDTH // N_POOL_GROUPS
POOL_HIST = max(POOL_WINDOWS) - 1
IN_WIDTH = ATTN_WIDTH + 2 * KV_WIDTH + POOL_WIDTH + 2 * D_MODEL
N_EXPERTS = 32
TOP_K = 4
D_FF = D_MODEL
SWIGLU_LIMIT = 7.0
SWIGLU_ALPHA = 1.702
MOE_BLOCK = 256
NORM_EPS = 1e-5

kernel_name = 'hybrid_swa_sink_pool_moe_stream_step'


def _rms_norm(x, g):
    xf = x.astype(jnp.float32)
    y = xf * lax.rsqrt(jnp.mean(xf * xf, axis=-1, keepdims=True) + NORM_EPS)
    return (y * g.astype(jnp.float32)).astype(x.dtype)


def _rope(x, pos):
    half = ROT_DIM // 2
    inv_freq = ROPE_THETA ** (-jnp.arange(0, ROT_DIM, 2, dtype=jnp.float32) / ROT_DIM)
    ang = pos.astype(jnp.float32)[:, None] * inv_freq[None, :]
    cos = jnp.cos(ang)[:, None, :]
    sin = jnp.sin(ang)[:, None, :]
    xr = x[..., :ROT_DIM].astype(jnp.float32)
    x1, x2 = xr[..., :half], xr[..., half:]
    rot = jnp.concatenate([x1 * cos - x2 * sin, x2 * cos + x1 * sin], axis=-1)
    return jnp.concatenate([rot.astype(x.dtype), x[..., ROT_DIM:]], axis=-1)


def _split_in(z):
    cuts = np.cumsum([ATTN_WIDTH, KV_WIDTH, KV_WIDTH, POOL_WIDTH, D_MODEL]).tolist()
    return jnp.split(z, cuts, axis=-1)


def _qk(q, k, pos, g_q, g_k):
    n, t = q.shape[:2]
    q = _rope(_rms_norm(q.reshape(n, t, N_Q_HEADS, HEAD_DIM), g_q), pos)
    k = _rope(_rms_norm(k.reshape(n, t, N_KV_HEADS, HEAD_DIM), g_k), pos)
    return q.reshape(n, t, N_KV_HEADS, Q_PER_KV, HEAD_DIM), k


def _sink_attention(q, k, v, sinks, valid):
    s = jnp.einsum('...qhgd,...khd->...hgqk', q, k, preferred_element_type=jnp.float32) * ATTN_SCALE
    if valid is not None:
        s = jnp.where(valid, s, NEG_INF)
    sink = sinks.astype(jnp.float32).reshape(N_KV_HEADS, Q_PER_KV, 1, 1)
    m = jnp.maximum(jnp.max(s, axis=-1, keepdims=True), sink)
    p = jnp.exp(s - m)
    denom = jnp.sum(p, axis=-1, keepdims=True) + jnp.exp(sink - m)
    return jnp.einsum('...hgqk,...khd->...qhgd', (p / denom).astype(v.dtype), v)


def _banded_window_attention(q, k, v, sinks):
    b, s = q.shape[:2]
    n_chunks = s // CHUNK
    qb = q.reshape(b, n_chunks, CHUNK, N_KV_HEADS, Q_PER_KV, HEAD_DIM)

    def band(t):
        tp = jnp.concatenate([jnp.zeros((b, WINDOW_CHUNKS * CHUNK, N_KV_HEADS, HEAD_DIM), t.dtype), t], axis=1)
        tp = tp.reshape(b, n_chunks + WINDOW_CHUNKS, CHUNK, N_KV_HEADS, HEAD_DIM)
        return jnp.concatenate([tp[:, j:j + n_chunks] for j in range(WINDOW_CHUNKS + 1)], axis=2)

    key_chunk = (jnp.arange(n_chunks)[:, None] - WINDOW_CHUNKS
                 + jnp.arange((WINDOW_CHUNKS + 1) * CHUNK)[None, :] // CHUNK)
    valid = (key_chunk >= 0)[:, None, None, None, :]
    o = _sink_attention(qb, band(k), band(v), sinks, valid)
    return o.reshape(b, s, ATTN_WIDTH)


def _pool_mixer(u, hist, pos, w_pool, pool_scale):
    n, t, p = u.shape
    ext = jnp.concatenate([hist.astype(u.dtype), u], axis=1)
    cs = jnp.cumsum(jnp.concatenate([jnp.zeros((n, 1, p), jnp.float32), ext.astype(jnp.float32)], axis=1), axis=1)
    end = cs[:, POOL_HIST + 1:POOL_HIST + 1 + t]
    parts = []
    for gi, w in enumerate(POOL_WINDOWS):
        sl = slice(gi * POOL_GROUP, (gi + 1) * POOL_GROUP)
        start = cs[:, POOL_HIST + 1 - w:POOL_HIST + 1 - w + t, sl]
        count = jnp.minimum(pos + 1, w).astype(jnp.float32)[None, :, None]
        parts.append((end[..., sl] - start) / count)
    pooled = jnp.concatenate(parts, axis=-1).astype(u.dtype)
    z = (pooled - u).reshape(n, t, N_POOL_GROUPS, POOL_GROUP)
    z = jnp.einsum('ntgc,gcd->ntgd', z, w_pool).reshape(n, t, p) * pool_scale
    return z, ext[:, -POOL_HIST:]


def _merge(att, pool, ga, gb, w_a, w_b, w_out):
    m = jax.nn.sigmoid(ga) * (att @ w_a) + jax.nn.sigmoid(gb) * (pool @ w_b)
    return m @ w_out


def _moe(h, w_router, b_router, w1, b1, w2, b2):
    t, d = h.shape
    logits = jnp.einsum('td,de->te', h, w_router, preferred_element_type=jnp.float32) + b_router.astype(jnp.float32)
    top_v, top_e = lax.top_k(logits, TOP_K)
    gates = jax.nn.softmax(top_v, axis=-1)
    e_flat = top_e.reshape(-1)
    tok_flat = jnp.repeat(jnp.arange(t, dtype=jnp.int32), TOP_K)
    g_flat = gates.reshape(-1)
    n_assign = t * TOP_K
    n_blocks = (n_assign + N_EXPERTS * (MOE_BLOCK - 1) + MOE_BLOCK - 1) // MOE_BLOCK
    n_rows = n_blocks * MOE_BLOCK
    order = jnp.argsort(e_flat)
    e_sorted = e_flat[order]
    counts = jnp.bincount(e_flat, length=N_EXPERTS)
    padded = (counts + MOE_BLOCK - 1) // MOE_BLOCK * MOE_BLOCK
    starts = jnp.cumsum(counts) - counts
    pad_ends = jnp.cumsum(padded)
    pad_starts = pad_ends - padded
    dest = pad_starts[e_sorted] + jnp.arange(n_assign, dtype=jnp.int32) - starts[e_sorted]
    row_tok = jnp.full((n_rows,), t, jnp.int32).at[dest].set(tok_flat[order])
    row_gate = jnp.zeros((n_rows,), jnp.float32).at[dest].set(g_flat[order])
    block_expert = jnp.minimum(
        jnp.searchsorted(pad_ends, jnp.arange(n_blocks, dtype=jnp.int32) * MOE_BLOCK, side='right'),
        N_EXPERTS - 1)
    h_pad = jnp.concatenate([h, jnp.zeros((1, d), h.dtype)], axis=0)

    def expert_block(args):
        tok, gate_w, e = args
        xb = h_pad[tok]
        hh = xb @ w1[e] + b1[e]
        gate = jnp.minimum(hh[:, :D_FF], SWIGLU_LIMIT)
        lin = jnp.clip(hh[:, D_FF:], -SWIGLU_LIMIT, SWIGLU_LIMIT)
        act = gate * jax.nn.sigmoid(SWIGLU_ALPHA * gate) * (lin + 1.0)
        return (act @ w2[e] + b2[e]) * gate_w[:, None].astype(xb.dtype)

    out = lax.map(expert_block, (row_tok.reshape(n_blocks, MOE_BLOCK), row_gate.reshape(n_blocks, MOE_BLOCK), block_expert))
    return jax.ops.segment_sum(out.reshape(n_rows, d), row_tok, num_segments=t + 1)[:t]


def _layer(xp, xs, ck, cv, sp, g1, w_in, g_q, g_k, sinks, w_pool, pool_scale, w_a, w_b, w_out,
           g2, w_router, b_router, w1, b1, w2, b2):
    b, s, d = xp.shape
    db, ds, _ = xs.shape
    q, k, v, u, ga, gb = _split_in(_rms_norm(xp, g1) @ w_in)
    pos = jnp.arange(s, dtype=jnp.int32)
    q, k = _qk(q, k, pos, g_q, g_k)
    v = v.reshape(b, s, N_KV_HEADS, HEAD_DIM)
    att = _banded_window_attention(q, k, v, sinks)
    pool, pool_state_p = _pool_mixer(u, jnp.zeros((b, POOL_HIST, POOL_WIDTH), u.dtype), pos, w_pool, pool_scale)
    x1p = xp + _merge(att, pool, ga, gb, w_a, w_b, w_out)
    k_state_p = k[:, -WINDOW:]
    v_state_p = v[:, -WINDOW:]
    q, k, v, u, ga, gb = _split_in(_rms_norm(xs, g1) @ w_in)
    pos = PAST_LEN + jnp.arange(ds, dtype=jnp.int32)
    q, k = _qk(q, k, pos, g_q, g_k)
    v = v.reshape(db, ds, N_KV_HEADS, HEAD_DIM)
    k_all = jnp.concatenate([ck.astype(k.dtype), k], axis=1)
    v_all = jnp.concatenate([cv.astype(v.dtype), v], axis=1)
    att = _sink_attention(q, k_all, v_all, sinks, None).reshape(db, ds, ATTN_WIDTH)
    pool, pool_state_s = _pool_mixer(u, sp, pos, w_pool, pool_scale)
    x1s = xs + _merge(att, pool, ga, gb, w_a, w_b, w_out)
    n_win = ck.shape[1]
    k_state_s = k_all[:, -n_win:]
    v_state_s = v_all[:, -n_win:]
    h2 = jnp.concatenate([_rms_norm(x1p, g2).reshape(b * s, d), _rms_norm(x1s, g2).reshape(db * ds, d)], axis=0)
    f = _moe(h2, w_router, b_router, w1, b1, w2, b2)
    yp = x1p + f[:b * s].reshape(b, s, d)
    ys = x1s + f[b * s:].reshape(db, ds, d)
    return yp, ys, (k_state_p, v_state_p, pool_state_p), (k_state_s, v_state_s, pool_state_s)


def setup_inputs(seed: int = 0) -> dict:
    key = jax.random.key(seed)
    ks = jax.random.split(key, 24)
    f32 = jnp.float32

    def nrm(k, shape, scale):
        return jax.random.normal(k, shape, f32) * scale

    n_win = min(WINDOW, PAST_LEN)
    L = DEPTH
    return {
        'x_prompt': nrm(ks[0], (BATCH, SEQ, D_MODEL), 1.0),
        'x_sample': nrm(ks[1], (DEC_BATCH, DEC_SEQ, D_MODEL), 1.0),
        'cache_k': nrm(ks[2], (L, DEC_BATCH, n_win, N_KV_HEADS, HEAD_DIM), 1.0),
        'cache_v': nrm(ks[3], (L, DEC_BATCH, n_win, N_KV_HEADS, HEAD_DIM), 1.0),
        'state_pool': nrm(ks[4], (L, DEC_BATCH, POOL_HIST, POOL_WIDTH), 1.0),
        'g_norm1': 1.0 + nrm(ks[5], (L, D_MODEL), 0.02),
        'w_in': nrm(ks[6], (L, D_MODEL, IN_WIDTH), D_MODEL ** -0.5),
        'g_q': 1.0 + nrm(ks[7], (L, HEAD_DIM), 0.02),
        'g_k': 1.0 + nrm(ks[8], (L, HEAD_DIM), 0.02),
        'attn_sinks': nrm(ks[9], (L, N_Q_HEADS), 1.0),
        'w_pool': nrm(ks[10], (L, N_POOL_GROUPS, POOL_GROUP, POOL_GROUP), POOL_GROUP ** -0.5),
        'pool_scale': 1.0 + nrm(ks[11], (L, POOL_WIDTH), 0.02),
        'w_a': nrm(ks[12], (L, ATTN_WIDTH, D_MODEL), ATTN_WIDTH ** -0.5),
        'w_b': nrm(ks[13], (L, POOL_WIDTH, D_MODEL), POOL_WIDTH ** -0.5),
        'w_out': nrm(ks[14], (L, D_MODEL, D_MODEL), D_MODEL ** -0.5),
        'g_norm2': 1.0 + nrm(ks[15], (L, D_MODEL), 0.02),
        'w_router': nrm(ks[16], (L, D_MODEL, N_EXPERTS), D_MODEL ** -0.5),
        'b_router': nrm(ks[17], (L, N_EXPERTS), 0.01),
        'w1': nrm(ks[18], (L, N_EXPERTS, D_MODEL, 2 * D_FF), D_MODEL ** -0.5),
        'b1': nrm(ks[19], (L, N_EXPERTS, 2 * D_FF), 0.01),
        'w2': nrm(ks[20], (L, N_EXPERTS, D_FF, D_MODEL), D_FF ** -0.5),
        'b2': nrm(ks[21], (L, N_EXPERTS, D_MODEL), 0.01),
    }


def reference(x_prompt, x_sample, cache_k, cache_v, state_pool, g_norm1, w_in, g_q, g_k, attn_sinks,
              w_pool, pool_scale, w_a, w_b, w_out, g_norm2, w_router, b_router, w1, b1, w2, b2):
    y_prompt, y_sample = x_prompt, x_sample
    nk_p, nv_p, np_p, nk_s, nv_s, np_s = [], [], [], [], [], []
    for l in range(DEPTH):
        y_prompt, y_sample, st_p, st_s = _layer(
            y_prompt, y_sample, cache_k[l], cache_v[l], state_pool[l], g_norm1[l], w_in[l], g_q[l], g_k[l],
            attn_sinks[l], w_pool[l], pool_scale[l], w_a[l], w_b[l], w_out[l], g_norm2[l], w_router[l],
            b_router[l], w1[l], b1[l], w2[l], b2[l])
        nk_p.append(st_p[0])
        nv_p.append(st_p[1])
        np_p.append(st_p[2])
        nk_s.append(st_s[0])
        nv_s.append(st_s[1])
        np_s.append(st_s[2])
    return (y_prompt, y_sample, jnp.stack(nk_p), jnp.stack(nv_p), jnp.stack(np_p),
            jnp.stack(nk_s), jnp.stack(nv_s), jnp.stack(np_s))
```

```python
import functools

import jax
import jax.numpy as jnp
import numpy as np
from jax import lax
from jax.experimental import pallas as pl
from jax.experimental.pallas import tpu as pltpu

F32 = jnp.float32
BF16 = jnp.bfloat16

CHUNK = 64
WINDOW = 128
HEAD_DIM = 64
ROT_DIM = HEAD_DIM // 4
ROPE_THETA = 500000.0
PAST_LEN = 2048
POOL_WINDOWS = (2, 4, 8, 16)
POOL_HIST = max(POOL_WINDOWS) - 1
POOL_PAD = 16
TOP_K = 4
SWIGLU_LIMIT = 7.0
SWIGLU_ALPHA = 1.702
NORM_EPS = 1e-5
NEG_INF = -1e30
LANES = 128

VMEM_LIMIT = 56 * 1024 * 1024


def _rms(x, g):
    ms = jnp.mean(x * x, axis=-1, keepdims=True)
    return x * lax.rsqrt(ms + NORM_EPS) * g


def _split_dot(a, b):
    hi = a.astype(BF16)
    lo = (a - hi.astype(F32)).astype(BF16)
    return (jnp.dot(hi, b, preferred_element_type=F32)
            + jnp.dot(lo, b, preferred_element_type=F32))


def _front_kernel(x_ref, g1_ref, w_ref, gqk_ref, sink_ref, cos_ref, sin_ref,
                  k0_ref, v0_ref, u0_ref, seg_ref, segt_ref, wpool_ref, pscale_ref,
                  att_ref, pool_ref, kst_ref, vst_ref, ust_ref,
                  kbuf, vbuf, ubuf, *, tm, chunk, pos0, n_q, n_kv):
    s = pl.program_id(1)
    qw = n_q * HEAD_DIM
    kw = n_kv * HEAD_DIM
    grp = n_q // n_kv

    @pl.when(s == 0)
    def _():
        kbuf[0:WINDOW, :] = k0_ref[0]
        vbuf[0:WINDOW, :] = v0_ref[0]
        ubuf[0:POOL_PAD, :] = u0_ref[0]

    xn = _rms(x_ref[0], g1_ref[...]).astype(BF16)
    z = jnp.dot(xn, w_ref[...], preferred_element_type=F32)
    qk = z[:, :qw + kw]
    v = z[:, qw + kw:qw + 2 * kw]
    u = z[:, qw + 2 * kw:]

    ss = _split_dot(qk * qk, seg_ref[...])
    r = lax.rsqrt(ss * (1.0 / HEAD_DIM) + NORM_EPS)
    qkn = qk * _split_dot(r, segt_ref[...]) * gqk_ref[...]

    cos = cos_ref[...]
    sin = sin_ref[...]
    lane = lax.broadcasted_iota(jnp.int32, (tm, LANES), 1)
    low_half = (lane % HEAD_DIM) < (ROT_DIM // 2)
    blocks = []
    for c in range((qw + kw) // LANES):
        blk = qkn[:, c * LANES:(c + 1) * LANES]
        swapped = jnp.where(low_half,
                            pltpu.roll(blk, LANES - ROT_DIM // 2, axis=1),
                            pltpu.roll(blk, ROT_DIM // 2, axis=1))
        blocks.append(blk * cos + swapped * sin)
    q_blocks = blocks[:qw // LANES]
    k_rot = jnp.concatenate(blocks[qw // LANES:], axis=1) if kw > LANES else blocks[qw // LANES]

    kbuf[WINDOW:WINDOW + tm, :] = k_rot
    vbuf[WINDOW:WINDOW + tm, :] = v
    ubuf[POOL_PAD:POOL_PAD + tm, :] = u

    tile_pos = pos0 + s * tm
    nk = WINDOW + chunk
    rows = grp * chunk
    for c in range(tm // chunk):
        k_all = kbuf[c * chunk:c * chunk + nk, :].astype(BF16)
        v_all = vbuf[c * chunk:c * chunk + nk, :].astype(BF16)
        if pos0 < WINDOW:
            jpos = lax.broadcasted_iota(jnp.int32, (rows, nk), 1)
            valid = (jpos + (tile_pos + c * chunk - WINDOW)) >= 0
        for h in range(n_kv):
            kh = k_all[:, h * HEAD_DIM:(h + 1) * HEAD_DIM]
            vh = v_all[:, h * HEAD_DIM:(h + 1) * HEAD_DIM]
            qs = []
            for g in range(grp):
                head = h * grp + g
                blk = q_blocks[(head * HEAD_DIM) // LANES]
                off = (head * HEAD_DIM) % LANES
                qs.append(blk[c * chunk:(c + 1) * chunk, off:off + HEAD_DIM])
            qh = jnp.concatenate(qs, axis=0).astype(BF16)
            sc = lax.dot_general(qh, kh, (((1,), (1,)), ((), ())),
                                 preferred_element_type=F32)
            if pos0 < WINDOW:
                sc = jnp.where(valid, sc, NEG_INF)
            sink = sink_ref[h]
            m = jnp.maximum(jnp.max(sc, axis=-1, keepdims=True), sink)
            p = jnp.exp(sc - m)
            denom = jnp.sum(p, axis=-1, keepdims=True) + jnp.exp(sink - m)
            o = jnp.dot(p.astype(BF16), vh, preferred_element_type=F32) / denom
            for g in range(grp):
                head = h * grp + g
                att_ref[0, c * chunk:(c + 1) * chunk, head * HEAD_DIM:(head + 1) * HEAD_DIM] = (
                    o[g * chunk:(g + 1) * chunk].astype(BF16))

    posf = (tile_pos + lax.broadcasted_iota(jnp.int32, (tm, 1), 0)).astype(F32)
    pg = u.shape[1] // len(POOL_WINDOWS)
    for gi, w in enumerate(POOL_WINDOWS):
        cols = slice(gi * pg, (gi + 1) * pg)
        acc = ubuf[POOL_PAD:POOL_PAD + tm, cols]
        for j in range(1, w):
            acc = acc + ubuf[POOL_PAD - j:POOL_PAD - j + tm, cols]
        pooled = acc / jnp.minimum(posf + 1.0, float(w))
        zg = (pooled - u[:, cols]).astype(BF16)
        og = jnp.dot(zg, wpool_ref[gi], preferred_element_type=F32) * pscale_ref[:, cols]
        pool_ref[0, :, cols] = og.astype(BF16)

    kst_ref[0] = kbuf[tm:tm + WINDOW, :]
    vst_ref[0] = vbuf[tm:tm + WINDOW, :]
    ust_ref[0] = ubuf[tm + POOL_PAD - POOL_HIST:tm + POOL_PAD, :]
    k_tail = kbuf[tm:tm + WINDOW, :]
    v_tail = vbuf[tm:tm + WINDOW, :]
    u_tail = ubuf[tm:tm + POOL_PAD, :]
    kbuf[0:WINDOW, :] = k_tail
    vbuf[0:WINDOW, :] = v_tail
    ubuf[0:POOL_PAD, :] = u_tail


def _front(x, g1, w_qkvu, gqk, sink_cols, cos_t, sin_t, k0, v0, u0, seg, segt, wpool, pscale,
           *, tm, chunk, pos0, n_q, n_kv, interpret):
    b, s, d = x.shape
    qw, kw = n_q * HEAD_DIM, n_kv * HEAD_DIM
    pw = w_qkvu.shape[1] - qw - 2 * kw
    n_s = s // tm
    full = lambda shape: pl.BlockSpec(shape, lambda i, j: (0,) * len(shape))
    per_b = lambda shape: pl.BlockSpec(shape, lambda i, j: (i,) + (0,) * (len(shape) - 1))
    kern = functools.partial(_front_kernel, tm=tm, chunk=chunk, pos0=pos0, n_q=n_q, n_kv=n_kv)
    return pl.pallas_call(
        kern,
        grid=(b, n_s),
        in_specs=[
            pl.BlockSpec((1, tm, d), lambda i, j: (i, j, 0)),
            full((1, d)),
            full(w_qkvu.shape),
            full((1, qw + kw)),
            full(sink_cols.shape),
            pl.BlockSpec((tm, LANES), lambda i, j: (j, 0)),
            pl.BlockSpec((tm, LANES), lambda i, j: (j, 0)),
            per_b((1, WINDOW, kw)),
            per_b((1, WINDOW, kw)),
            per_b((1, POOL_PAD, pw)),
            full(seg.shape),
            full(segt.shape),
            full(wpool.shape),
            full((1, pw)),
        ],
        out_specs=[
            pl.BlockSpec((1, tm, qw), lambda i, j: (i, j, 0)),
            pl.BlockSpec((1, tm, pw), lambda i, j: (i, j, 0)),
            per_b((1, WINDOW, kw)),
            per_b((1, WINDOW, kw)),
            per_b((1, POOL_HIST, pw)),
        ],
        out_shape=[
            jax.ShapeDtypeStruct((b, s, qw), BF16),
            jax.ShapeDtypeStruct((b, s, pw), BF16),
            jax.ShapeDtypeStruct((b, WINDOW, kw), F32),
            jax.ShapeDtypeStruct((b, WINDOW, kw), F32),
            jax.ShapeDtypeStruct((b, POOL_HIST, pw), F32),
        ],
        scratch_shapes=[
            pltpu.VMEM((WINDOW + tm, kw), F32),
            pltpu.VMEM((WINDOW + tm, kw), F32),
            pltpu.VMEM((POOL_PAD + tm, pw), F32),
        ],
        compiler_params=pltpu.CompilerParams(
            dimension_semantics=("arbitrary", "arbitrary"), vmem_limit_bytes=VMEM_LIMIT),
        name="front",
        interpret=interpret,
    )(x, g1, w_qkvu, gqk, sink_cols, cos_t, sin_t, k0, v0, u0, seg, segt, wpool, pscale)


def _merge_kernel(x_ref, att_ref, pool_ref, g1_ref, wg_ref, wa_ref, wb_ref, wo_ref, g2_ref,
                  wr_ref, br_ref, x1_ref, h2_ref, tope_ref, gate_ref, *, n_exp):
    x = x_ref[...]
    d = x.shape[1]
    tm = x.shape[0]
    xn = _rms(x, g1_ref[...]).astype(BF16)
    ga = jnp.dot(xn, wg_ref[:, :d], preferred_element_type=F32)
    m = jax.nn.sigmoid(ga) * jnp.dot(att_ref[...], wa_ref[...], preferred_element_type=F32)
    gb = jnp.dot(xn, wg_ref[:, d:], preferred_element_type=F32)
    m = m + jax.nn.sigmoid(gb) * jnp.dot(pool_ref[...], wb_ref[...], preferred_element_type=F32)
    x1 = x + jnp.dot(m.astype(BF16), wo_ref[...], preferred_element_type=F32)
    x1_ref[...] = x1
    h2 = _rms(x1, g2_ref[...]).astype(BF16)
    h2_ref[...] = h2

    logits = lax.dot_general(wr_ref[...], h2, (((1,), (1,)), ((), ())),
                             preferred_element_type=F32) + br_ref[...]
    idx = lax.broadcasted_iota(jnp.int32, (n_exp, tm), 0)
    vals, ids = [], []
    for _ in range(TOP_K):
        mk = jnp.max(logits, axis=0, keepdims=True)
        ik = jnp.min(jnp.where(logits == mk, idx, n_exp), axis=0, keepdims=True)
        vals.append(mk)
        ids.append(ik)
        logits = jnp.where(idx == ik, -jnp.inf, logits)
    es = [jnp.exp(vk - vals[0]) for vk in vals]
    tot = es[0]
    for ek in es[1:]:
        tot = tot + ek
    tope_ref[...] = jnp.concatenate(ids, axis=0)
    gate_ref[...] = jnp.concatenate([ek / tot for ek in es], axis=0)


def _merge(x, att, pool, g1, wg, wa, wb, wo, g2, wr, br, *, tm, interpret):
    t, d = x.shape
    n_exp = wr.shape[0]
    n_t = t // tm
    once = pl.Buffered(1)
    full = lambda a: pl.BlockSpec(a.shape, lambda i: (0,) * a.ndim, pipeline_mode=once)
    row = lambda w: pl.BlockSpec((tm, w), lambda i: (i, 0))
    return pl.pallas_call(
        functools.partial(_merge_kernel, n_exp=n_exp),
        grid=(n_t,),
        in_specs=[row(d), row(att.shape[1]), row(pool.shape[1]), full(g1), full(wg), full(wa),
                  full(wb), full(wo), full(g2), full(wr), full(br)],
        out_specs=[row(d), row(d),
                   pl.BlockSpec((TOP_K, tm), lambda i: (0, i)),
                   pl.BlockSpec((TOP_K, tm), lambda i: (0, i))],
        out_shape=[jax.ShapeDtypeStruct((t, d), F32), jax.ShapeDtypeStruct((t, d), BF16),
                   jax.ShapeDtypeStruct((TOP_K, t), jnp.int32),
                   jax.ShapeDtypeStruct((TOP_K, t), F32)],
        compiler_params=pltpu.CompilerParams(
            dimension_semantics=("arbitrary",), vmem_limit_bytes=VMEM_LIMIT),
        name="merge",
        interpret=interpret,
    )(x, att, pool, g1, wg, wa, wb, wo, g2, wr, br)


def _moe_kernel(te_ref, tv_ref, xs_ref, w1g_ref, w1l_ref, b1g_ref, b1l_ref, w2_ref, b2_ref,
                o_ref):
    m = pl.program_id(0)
    f = pl.program_id(1)
    valid = tv_ref[m] > 0

    @pl.when(valid)
    def _():
        x = xs_ref[...]
        hg = jnp.dot(x, w1g_ref[0].astype(BF16), preferred_element_type=F32) + b1g_ref[0]
        hl = jnp.dot(x, w1l_ref[0].astype(BF16), preferred_element_type=F32) + b1l_ref[0]
        gate = jnp.minimum(hg, SWIGLU_LIMIT)
        lin = jnp.clip(hl, -SWIGLU_LIMIT, SWIGLU_LIMIT)
        act = gate * jax.nn.sigmoid(SWIGLU_ALPHA * gate) * (lin + 1.0)
        contrib = jnp.dot(act.astype(BF16), w2_ref[0].astype(BF16), preferred_element_type=F32)

        @pl.when(f == 0)
        def _():
            o_ref[...] = contrib + b2_ref[0]

        @pl.when(f > 0)
        def _():
            o_ref[...] += contrib

    @pl.when(jnp.logical_and(jnp.logical_not(valid), f == 0))
    def _():
        o_ref[...] = jnp.zeros_like(o_ref)


def _moe(tile_expert, tile_valid, xs, w1, b1, w2, b2, *, tme, tf, interpret):
    n_rows, d = xs.shape
    n_exp, _, ff2 = w1.shape
    ff = ff2 // 2
    n_f = ff // tf
    n_tiles = n_rows // tme
    b1r = b1.reshape(n_exp, 1, ff2)
    b2r = b2.reshape(n_exp, 1, d)
    grid_spec = pltpu.PrefetchScalarGridSpec(
        num_scalar_prefetch=2,
        grid=(n_tiles, n_f),
        in_specs=[
            pl.BlockSpec((tme, d), lambda m, f, te, tv: (m, 0)),
            pl.BlockSpec((1, d, tf), lambda m, f, te, tv: (te[m], 0, f)),
            pl.BlockSpec((1, d, tf), lambda m, f, te, tv: (te[m], 0, n_f + f)),
            pl.BlockSpec((1, 1, tf), lambda m, f, te, tv: (te[m], 0, f)),
            pl.BlockSpec((1, 1, tf), lambda m, f, te, tv: (te[m], 0, n_f + f)),
            pl.BlockSpec((1, tf, d), lambda m, f, te, tv: (te[m], f, 0)),
            pl.BlockSpec((1, 1, d), lambda m, f, te, tv: (te[m], 0, 0)),
        ],
        out_specs=pl.BlockSpec((tme, d), lambda m, f, te, tv: (m, 0)),
    )
    return pl.pallas_call(
        _moe_kernel,
        grid_spec=grid_spec,
        out_shape=jax.ShapeDtypeStruct((n_rows, d), F32),
        compiler_params=pltpu.CompilerParams(
            dimension_semantics=("arbitrary", "arbitrary"), vmem_limit_bytes=VMEM_LIMIT),
        name="moe",
        interpret=interpret,
    )(tile_expert, tile_valid, xs, w1, w1, b1r, b1r, w2, b2r)


def _rope_tables(pos):
    half = ROT_DIM // 2
    inv_freq = ROPE_THETA ** (-jnp.arange(0, ROT_DIM, 2, dtype=F32) / ROT_DIM)
    ang = pos.astype(F32)[:, None] * inv_freq[None, :]
    cos, sin = jnp.cos(ang), jnp.sin(ang)
    n = pos.shape[0]
    pad = HEAD_DIM - ROT_DIM
    cos_h = jnp.concatenate([cos, cos, jnp.ones((n, pad), F32)], axis=1)
    sin_h = jnp.concatenate([-sin, sin, jnp.zeros((n, pad), F32)], axis=1)
    reps = LANES // HEAD_DIM
    return jnp.tile(cos_h, (1, reps)), jnp.tile(sin_h, (1, reps))


def _routing(tope, n_exp, tme):
    t = tope.shape[1]
    n_assign = t * TOP_K
    e_flat = tope.T.reshape(-1)
    onehot = (e_flat[:, None] == jnp.arange(n_exp, dtype=jnp.int32)[None, :]).astype(jnp.int32)
    csum = jnp.cumsum(onehot, axis=0)
    counts = csum[-1]
    padded = (counts + tme - 1) // tme * tme
    pad_ends = jnp.cumsum(padded)
    pad_starts = pad_ends - padded
    dest = jnp.sum(onehot * (csum - 1 + pad_starts[None, :]), axis=1)
    n_tiles = (n_assign + n_exp * (tme - 1) + tme - 1) // tme
    tile_start = jnp.arange(n_tiles, dtype=jnp.int32) * tme
    tile_valid = (tile_start < pad_ends[-1]).astype(jnp.int32)
    last_valid = jnp.maximum(pad_ends[-1] // tme - 1, 0)
    tile_expert = jnp.searchsorted(pad_ends, jnp.minimum(tile_start, last_valid * tme),
                                   side='right').astype(jnp.int32)
    tile_expert = jnp.minimum(tile_expert, n_exp - 1)
    return dest, tile_expert, tile_valid, n_tiles


def _layer(xp, xs, ck, cv, sp, g1, w_in, g_q, g_k, sinks, w_pool, pool_scale, w_a, w_b, w_out,
           g2, w_router, b_router, w1, b1, w2, b2, *, tm_p, tm_merge, tme, tf, interpret):
    b, s, d = xp.shape
    db, ds, _ = xs.shape
    n_q = sinks.shape[0]
    n_kv = ck.shape[2]
    grp = n_q // n_kv
    qw, kw = n_q * HEAD_DIM, n_kv * HEAD_DIM
    pw = pool_scale.shape[0]
    n_exp = w_router.shape[1]
    split = qw + 2 * kw + pw

    w_qkvu = w_in[:, :split].astype(BF16)
    w_gates = w_in[:, split:].astype(BF16)
    g1r = g1.reshape(1, d)
    g2r = g2.reshape(1, d)
    gqk = jnp.concatenate([jnp.tile(g_q, n_q) * (HEAD_DIM ** -0.5), jnp.tile(g_k, n_kv)]).reshape(1, qw + kw)
    head_of_lane = jnp.arange(qw + kw, dtype=jnp.int32) // HEAD_DIM
    seg = (head_of_lane[:, None] == jnp.arange(LANES, dtype=jnp.int32)[None, :]).astype(BF16)
    segt = seg.T
    wpool = w_pool.astype(BF16)
    pscale = pool_scale.reshape(1, pw)

    def sink_cols(chunk):
        return jnp.repeat(sinks.reshape(n_kv, grp), chunk, axis=1).reshape(n_kv, grp * chunk, 1)

    front = functools.partial(_front, n_q=n_q, n_kv=n_kv, interpret=interpret)
    cos_p, sin_p = _rope_tables(jnp.arange(s, dtype=jnp.int32))
    att_p, pool_p, k_st_p, v_st_p, u_st_p = front(
        xp, g1r, w_qkvu, gqk, sink_cols(CHUNK), cos_p, sin_p,
        jnp.zeros((b, WINDOW, kw), F32), jnp.zeros((b, WINDOW, kw), F32),
        jnp.zeros((b, POOL_PAD, pw), F32), seg, segt, wpool, pscale,
        tm=tm_p, chunk=CHUNK, pos0=0)
    n_win = ck.shape[1]
    cos_s, sin_s = _rope_tables(PAST_LEN + jnp.arange(ds, dtype=jnp.int32))
    u0 = jnp.concatenate([jnp.zeros((db, POOL_PAD - POOL_HIST, pw), F32), sp], axis=1)
    att_s, pool_s, k_st_s, v_st_s, u_st_s = front(
        xs, g1r, w_qkvu, gqk, sink_cols(ds), cos_s, sin_s,
        ck.reshape(db, n_win, kw), cv.reshape(db, n_win, kw), u0, seg, segt, wpool, pscale,
        tm=ds, chunk=ds, pos0=PAST_LEN)

    wa, wb, wo = w_a.astype(BF16), w_b.astype(BF16), w_out.astype(BF16)
    wr = w_router.T.astype(BF16)
    br = b_router.reshape(n_exp, 1)
    merge = functools.partial(_merge, interpret=interpret)
    x1p, h2p, tope_p, gate_p = merge(
        xp.reshape(b * s, d), att_p.reshape(b * s, qw), pool_p.reshape(b * s, pw),
        g1r, w_gates, wa, wb, wo, g2r, wr, br, tm=tm_merge)
    x1s, h2s, tope_s, gate_s = merge(
        xs.reshape(db * ds, d), att_s.reshape(db * ds, qw), pool_s.reshape(db * ds, pw),
        g1r, w_gates, wa, wb, wo, g2r, wr, br, tm=db * ds)

    h2 = jnp.concatenate([h2p, h2s], axis=0)
    tope = jnp.concatenate([tope_p, tope_s], axis=1)
    gates = jnp.concatenate([gate_p, gate_s], axis=1)
    t = h2.shape[0]
    dest, tile_expert, tile_valid, n_tiles = _routing(tope, n_exp, tme)
    row_tok = jnp.zeros((n_tiles * tme,), jnp.int32).at[dest].set(
        jnp.arange(t * TOP_K, dtype=jnp.int32) // TOP_K)
    rows = h2[row_tok]
    out = _moe(tile_expert, tile_valid, rows, w1, b1, w2, b2, tme=tme, tf=tf, interpret=interpret)
    f = jnp.einsum('tkd,tk->td', out[dest.reshape(t, TOP_K)], gates.T)
    yp = x1p + f[:b * s]
    ys = x1s + f[b * s:]
    n_heads_shape = (n_win, n_kv, HEAD_DIM)
    return (yp.reshape(b, s, d), ys.reshape(db, ds, d),
            k_st_p.reshape(b, WINDOW, n_kv, HEAD_DIM), v_st_p.reshape(b, WINDOW, n_kv, HEAD_DIM), u_st_p,
            k_st_s.reshape((db,) + n_heads_shape), v_st_s.reshape((db,) + n_heads_shape), u_st_s)


def _forward(x_prompt, x_sample, cache_k, cache_v, state_pool, g_norm1, w_in, g_q, g_k, attn_sinks,
             w_pool, pool_scale, w_a, w_b, w_out, g_norm2, w_router, b_router, w1, b1, w2, b2,
             *, tm_p=256, tm_merge=256, tme=1024, tf=256, interpret=False):
    depth = w_in.shape[0]
    yp, ys = x_prompt, x_sample
    outs = [[] for _ in range(6)]
    for l in range(depth):
        yp, ys, *st = _layer(
            yp, ys, cache_k[l], cache_v[l], state_pool[l], g_norm1[l], w_in[l], g_q[l], g_k[l],
            attn_sinks[l], w_pool[l], pool_scale[l], w_a[l], w_b[l], w_out[l], g_norm2[l],
            w_router[l], b_router[l], w1[l], b1[l], w2[l], b2[l],
            tm_p=tm_p, tm_merge=tm_merge, tme=tme, tf=tf, interpret=interpret)
        for acc, val in zip(outs, st):
            acc.append(val)
    return (yp, ys) + tuple(jnp.stack(o) for o in outs)


def kernel(x_prompt, x_sample, cache_k, cache_v, state_pool, g_norm1, w_in, g_q, g_k, attn_sinks,
           w_pool, pool_scale, w_a, w_b, w_out, g_norm2, w_router, b_router, w1, b1, w2, b2):
    return _forward(x_prompt, x_sample, cache_k, cache_v, state_pool, g_norm1, w_in, g_q, g_k,
                    attn_sinks, w_pool, pool_scale, w_a, w_b, w_out, g_norm2, w_router, b_router,
                    w1, b1, w2, b2)
```

```python
import functools

import jax
import jax.numpy as jnp
import numpy as np
from jax import lax
from jax.experimental import pallas as pl
from jax.experimental.pallas import tpu as pltpu

F32 = jnp.float32
BF16 = jnp.bfloat16

CHUNK = 64
WINDOW = 128
HEAD_DIM = 64
ROT_DIM = HEAD_DIM // 4
ROPE_THETA = 500000.0
PAST_LEN = 2048
POOL_WINDOWS = (2, 4, 8, 16)
POOL_HIST = max(POOL_WINDOWS) - 1
POOL_PAD = 16
TOP_K = 4
SWIGLU_LIMIT = 7.0
SWIGLU_ALPHA = 1.702
NORM_EPS = 1e-5
NEG_INF = -1e30
LANES = 128
SUBLANES = 8

VMEM_LIMIT = 56 * 1024 * 1024


def _rms(x, g):
    ms = jnp.mean(x * x, axis=-1, keepdims=True)
    return x * lax.rsqrt(ms + NORM_EPS) * g


def _pack_halves(h):
    half = h.shape[1] // 2
    lo = lax.bitcast_convert_type(h[:, :half].astype(F32), jnp.uint32)
    hi = lax.bitcast_convert_type(h[:, half:].astype(F32), jnp.uint32)
    return (hi & jnp.uint32(0xFFFF0000)) | (lo >> 16)


def _unpack_halves(w):
    lo = lax.bitcast_convert_type(w << 16, F32).astype(BF16)
    hi = lax.bitcast_convert_type(w & jnp.uint32(0xFFFF0000), F32).astype(BF16)
    return jnp.concatenate([lo, hi], axis=1)


def _split_dot(a, b):
    hi = a.astype(BF16)
    lo = (a - hi.astype(F32)).astype(BF16)
    return (jnp.dot(hi, b, preferred_element_type=F32)
            + jnp.dot(lo, b, preferred_element_type=F32))


def _front_kernel(x_ref, g1_ref, w_ref, gqk_ref, sink_ref, cos_ref, sin_ref,
                  k0_ref, v0_ref, u0_ref, seg_ref, segt_ref, wpool_ref, pscale_ref,
                  att_ref, pool_ref, kst_ref, vst_ref, ust_ref,
                  kbuf, vbuf, ubuf, *, tm, chunk, pos0, n_q, n_kv):
    s = pl.program_id(1)
    qw = n_q * HEAD_DIM
    kw = n_kv * HEAD_DIM
    grp = n_q // n_kv

    @pl.when(s == 0)
    def _():
        kbuf[0:WINDOW, :] = k0_ref[0]
        vbuf[0:WINDOW, :] = v0_ref[0]
        ubuf[0:POOL_PAD, :] = u0_ref[0]

    xn = _rms(x_ref[0], g1_ref[...]).astype(BF16)
    z = jnp.dot(xn, w_ref[...], preferred_element_type=F32)
    qk = z[:, :qw + kw]
    v = z[:, qw + kw:qw + 2 * kw]
    u = z[:, qw + 2 * kw:]

    ss = _split_dot(qk * qk, seg_ref[...])
    r = lax.rsqrt(ss * (1.0 / HEAD_DIM) + NORM_EPS)
    qkn = qk * _split_dot(r, segt_ref[...]) * gqk_ref[...]

    cos = cos_ref[...]
    sin = sin_ref[...]
    lane = lax.broadcasted_iota(jnp.int32, (tm, LANES), 1)
    low_half = (lane % HEAD_DIM) < (ROT_DIM // 2)
    blocks = []
    for c in range((qw + kw) // LANES):
        blk = qkn[:, c * LANES:(c + 1) * LANES]
        swapped = jnp.where(low_half,
                            pltpu.roll(blk, LANES - ROT_DIM // 2, axis=1),
                            pltpu.roll(blk, ROT_DIM // 2, axis=1))
        blocks.append(blk * cos + swapped * sin)
    q_blocks = blocks[:qw // LANES]
    k_rot = jnp.concatenate(blocks[qw // LANES:], axis=1) if kw > LANES else blocks[qw // LANES]

    kbuf[WINDOW:WINDOW + tm, :] = k_rot
    vbuf[WINDOW:WINDOW + tm, :] = v
    ubuf[POOL_PAD:POOL_PAD + tm, :] = u

    tile_pos = pos0 + s * tm
    nk = WINDOW + chunk
    rows = grp * chunk
    for c in range(tm // chunk):
        k_all = kbuf[c * chunk:c * chunk + nk, :].astype(BF16)
        v_all = vbuf[c * chunk:c * chunk + nk, :].astype(BF16)
        if pos0 < WINDOW:
            jpos = lax.broadcasted_iota(jnp.int32, (rows, nk), 1)
            valid = (jpos + (tile_pos + c * chunk - WINDOW)) >= 0
        for h in range(n_kv):
            kh = k_all[:, h * HEAD_DIM:(h + 1) * HEAD_DIM]
            vh = v_all[:, h * HEAD_DIM:(h + 1) * HEAD_DIM]
            qs = []
            for g in range(grp):
                head = h * grp + g
                blk = q_blocks[(head * HEAD_DIM) // LANES]
                off = (head * HEAD_DIM) % LANES
                qs.append(blk[c * chunk:(c + 1) * chunk, off:off + HEAD_DIM])
            qh = jnp.concatenate(qs, axis=0).astype(BF16)
            sc = lax.dot_general(qh, kh, (((1,), (1,)), ((), ())),
                                 preferred_element_type=F32)
            if pos0 < WINDOW:
                sc = jnp.where(valid, sc, NEG_INF)
            sink = sink_ref[h]
            m = jnp.maximum(jnp.max(sc, axis=-1, keepdims=True), sink)
            p = jnp.exp(sc - m)
            denom = jnp.sum(p, axis=-1, keepdims=True) + jnp.exp(sink - m)
            o = jnp.dot(p.astype(BF16), vh, preferred_element_type=F32) / denom
            for g in range(grp):
                head = h * grp + g
                att_ref[0, c * chunk:(c + 1) * chunk, head * HEAD_DIM:(head + 1) * HEAD_DIM] = (
                    o[g * chunk:(g + 1) * chunk].astype(BF16))

    posf = (tile_pos + lax.broadcasted_iota(jnp.int32, (tm, 1), 0)).astype(F32)
    pg = u.shape[1] // len(POOL_WINDOWS)
    for gi, w in enumerate(POOL_WINDOWS):
        cols = slice(gi * pg, (gi + 1) * pg)
        acc = ubuf[POOL_PAD:POOL_PAD + tm, cols]
        for j in range(1, w):
            acc = acc + ubuf[POOL_PAD - j:POOL_PAD - j + tm, cols]
        pooled = acc / jnp.minimum(posf + 1.0, float(w))
        zg = (pooled - u[:, cols]).astype(BF16)
        og = jnp.dot(zg, wpool_ref[gi], preferred_element_type=F32) * pscale_ref[:, cols]
        pool_ref[0, :, cols] = og.astype(BF16)

    kst_ref[0] = kbuf[tm:tm + WINDOW, :]
    vst_ref[0] = vbuf[tm:tm + WINDOW, :]
    ust_ref[0] = ubuf[tm + POOL_PAD - POOL_HIST:tm + POOL_PAD, :]
    k_tail = kbuf[tm:tm + WINDOW, :]
    v_tail = vbuf[tm:tm + WINDOW, :]
    u_tail = ubuf[tm:tm + POOL_PAD, :]
    kbuf[0:WINDOW, :] = k_tail
    vbuf[0:WINDOW, :] = v_tail
    ubuf[0:POOL_PAD, :] = u_tail


def _front(x, g1, w_qkvu, gqk, sink_cols, cos_t, sin_t, k0, v0, u0, seg, segt, wpool, pscale,
           *, tm, chunk, pos0, n_q, n_kv, interpret):
    b, s, d = x.shape
    qw, kw = n_q * HEAD_DIM, n_kv * HEAD_DIM
    pw = w_qkvu.shape[1] - qw - 2 * kw
    n_s = s // tm
    full = lambda shape: pl.BlockSpec(shape, lambda i, j: (0,) * len(shape))
    per_b = lambda shape: pl.BlockSpec(shape, lambda i, j: (i,) + (0,) * (len(shape) - 1))
    kern = functools.partial(_front_kernel, tm=tm, chunk=chunk, pos0=pos0, n_q=n_q, n_kv=n_kv)
    return pl.pallas_call(
        kern,
        grid=(b, n_s),
        in_specs=[
            pl.BlockSpec((1, tm, d), lambda i, j: (i, j, 0)),
            full((1, d)),
            full(w_qkvu.shape),
            full((1, qw + kw)),
            full(sink_cols.shape),
            pl.BlockSpec((tm, LANES), lambda i, j: (j, 0)),
            pl.BlockSpec((tm, LANES), lambda i, j: (j, 0)),
            per_b((1, WINDOW, kw)),
            per_b((1, WINDOW, kw)),
            per_b((1, POOL_PAD, pw)),
            full(seg.shape),
            full(segt.shape),
            full(wpool.shape),
            full((1, pw)),
        ],
        out_specs=[
            pl.BlockSpec((1, tm, qw), lambda i, j: (i, j, 0)),
            pl.BlockSpec((1, tm, pw), lambda i, j: (i, j, 0)),
            per_b((1, WINDOW, kw)),
            per_b((1, WINDOW, kw)),
            per_b((1, POOL_HIST, pw)),
        ],
        out_shape=[
            jax.ShapeDtypeStruct((b, s, qw), BF16),
            jax.ShapeDtypeStruct((b, s, pw), BF16),
            jax.ShapeDtypeStruct((b, WINDOW, kw), F32),
            jax.ShapeDtypeStruct((b, WINDOW, kw), F32),
            jax.ShapeDtypeStruct((b, POOL_HIST, pw), F32),
        ],
        scratch_shapes=[
            pltpu.VMEM((WINDOW + tm, kw), F32),
            pltpu.VMEM((WINDOW + tm, kw), F32),
            pltpu.VMEM((POOL_PAD + tm, pw), F32),
        ],
        compiler_params=pltpu.CompilerParams(
            dimension_semantics=("arbitrary", "arbitrary"), vmem_limit_bytes=VMEM_LIMIT),
        name="front",
        interpret=interpret,
    )(x, g1, w_qkvu, gqk, sink_cols, cos_t, sin_t, k0, v0, u0, seg, segt, wpool, pscale)


def _merge_kernel(x_ref, att_ref, pool_ref, g1_ref, wg_ref, wa_ref, wb_ref, wo_ref, g2_ref,
                  wr_ref, br_ref, x1_ref, h2_ref, tope_ref, gate_ref, *, n_exp):
    x = x_ref[...]
    d = x.shape[1]
    tm = x.shape[0]
    xn = _rms(x, g1_ref[...]).astype(BF16)
    ga = jnp.dot(xn, wg_ref[:, :d], preferred_element_type=F32)
    m = jax.nn.sigmoid(ga) * jnp.dot(att_ref[...], wa_ref[...], preferred_element_type=F32)
    gb = jnp.dot(xn, wg_ref[:, d:], preferred_element_type=F32)
    m = m + jax.nn.sigmoid(gb) * jnp.dot(pool_ref[...], wb_ref[...], preferred_element_type=F32)
    x1 = x + jnp.dot(m.astype(BF16), wo_ref[...], preferred_element_type=F32)
    x1_ref[...] = x1
    h2 = _rms(x1, g2_ref[...]).astype(BF16)
    h2_ref[...] = _pack_halves(h2)

    logits = lax.dot_general(wr_ref[...], h2, (((1,), (1,)), ((), ())),
                             preferred_element_type=F32) + br_ref[...]
    idx = lax.broadcasted_iota(jnp.int32, (n_exp, tm), 0)
    vals, ids = [], []
    for _ in range(TOP_K):
        mk = jnp.max(logits, axis=0, keepdims=True)
        ik = jnp.min(jnp.where(logits == mk, idx, n_exp), axis=0, keepdims=True)
        vals.append(mk)
        ids.append(ik)
        logits = jnp.where(idx == ik, -jnp.inf, logits)
    es = [jnp.exp(vk - vals[0]) for vk in vals]
    tot = es[0]
    for ek in es[1:]:
        tot = tot + ek
    tope_ref[...] = jnp.concatenate(ids, axis=0)
    gate_ref[...] = jnp.concatenate([ek / tot for ek in es], axis=0)


def _merge(x, att, pool, g1, wg, wa, wb, wo, g2, wr, br, *, tm, interpret):
    t, d = x.shape
    n_exp = wr.shape[0]
    n_t = t // tm
    once = pl.Buffered(1)
    full = lambda a: pl.BlockSpec(a.shape, lambda i: (0,) * a.ndim, pipeline_mode=once)
    row = lambda w: pl.BlockSpec((tm, w), lambda i: (i, 0))
    return pl.pallas_call(
        functools.partial(_merge_kernel, n_exp=n_exp),
        grid=(n_t,),
        in_specs=[row(d), row(att.shape[1]), row(pool.shape[1]), full(g1), full(wg), full(wa),
                  full(wb), full(wo), full(g2), full(wr), full(br)],
        out_specs=[row(d), row(d // 2),
                   pl.BlockSpec((TOP_K, tm), lambda i: (0, i)),
                   pl.BlockSpec((TOP_K, tm), lambda i: (0, i))],
        out_shape=[jax.ShapeDtypeStruct((t, d), F32), jax.ShapeDtypeStruct((t, d // 2), jnp.uint32),
                   jax.ShapeDtypeStruct((TOP_K, t), jnp.int32),
                   jax.ShapeDtypeStruct((TOP_K, t), F32)],
        compiler_params=pltpu.CompilerParams(
            dimension_semantics=("arbitrary",), vmem_limit_bytes=VMEM_LIMIT),
        name="merge",
        interpret=interpret,
    )(x, att, pool, g1, wg, wa, wb, wo, g2, wr, br)


def _row(ref, r):
    return ref.at[pl.ds(r, 1), :]


def _scatter_kernel(cnt_ref, ps_ref, pe_ref, dest_ref, hp_hbm, hs_hbm, xs_hbm, sem, *, tc, n_p, tme):
    i = pl.program_id(0)
    n = pl.num_programs(0)
    slot = lax.rem(i, 2)
    fill_sem = sem.at[2]
    sizes = [p for p in (tme >> (b + 1) for b in range(tme.bit_length() - 1)) if p >= SUBLANES]

    @pl.when(i == 0)
    def _():
        def fill(lo, hi, wait):
            def go(cp):
                if wait:
                    cp.wait()
                else:
                    cp.start()

            lo_al = (lo + (SUBLANES - 1)) // SUBLANES * SUBLANES
            for r in range(SUBLANES - 1):
                @pl.when(lo + r < lo_al)
                def _():
                    go(pltpu.make_async_copy(_row(hp_hbm, 0), _row(xs_hbm, lo + r), fill_sem))

            pos = lo_al
            for p in sizes:
                take = ((hi - lo_al) & p) > 0

                @pl.when(take)
                def _():
                    go(pltpu.make_async_copy(
                        hp_hbm.at[pl.ds(0, p), :],
                        xs_hbm.at[pl.ds(pl.multiple_of(pos, SUBLANES), p), :], fill_sem))

                pos = pos + jnp.where(take, p, 0)

        def per_expert(e, c):
            lo = ps_ref[e] + cnt_ref[e]
            fill(lo, pe_ref[e], False)
            fill(lo, pe_ref[e], True)
            return c

        n_exp = cnt_ref.shape[0]
        lax.fori_loop(0, n_exp, per_expert, 0)

        def per_tile(m, c):
            cp = pltpu.make_async_copy(hp_hbm.at[pl.ds(0, tme), :],
                                       xs_hbm.at[pl.ds(pl.multiple_of(m * tme, tme), tme), :], fill_sem)
            cp.start()
            cp.wait()
            return c

        lax.fori_loop(pe_ref[n_exp - 1] // tme, xs_hbm.shape[0] // tme, per_tile, 0)

    def issue(src, base):
        def body(j, c):
            for k in range(TOP_K):
                r = dest_ref[0, 0, j * TOP_K + k]
                pltpu.make_async_copy(_row(src, base + j), _row(xs_hbm, r), sem.at[slot]).start()
            return c

        lax.fori_loop(0, tc, body, 0, unroll=4)

    @pl.when(i < n_p)
    def _():
        issue(hp_hbm, i * tc)

    @pl.when(i >= n_p)
    def _():
        issue(hs_hbm, (i - n_p) * tc)

    def wait_chunk(sl):
        for _ in range(tc * TOP_K):
            pltpu.make_async_copy(_row(hp_hbm, 0), _row(xs_hbm, 0), sem.at[sl]).wait()

    @pl.when(i > 0)
    def _():
        wait_chunk(1 - slot)

    @pl.when(i == n - 1)
    def _():
        wait_chunk(slot)


def _scatter_rows(counts, pad_starts, pad_ends, dest, h2p, h2s, n_rows, *, tc, tme, interpret):
    t_p, half = h2p.shape
    t_s = h2s.shape[0]
    assert t_p >= tme and tme & (tme - 1) == 0 and n_rows % tme == 0
    n_p, n_s = t_p // tc, t_s // tc
    dest3 = dest.reshape(n_p + n_s, 1, tc * TOP_K)
    grid_spec = pltpu.PrefetchScalarGridSpec(
        num_scalar_prefetch=3,
        grid=(n_p + n_s,),
        in_specs=[
            pl.BlockSpec((1, 1, tc * TOP_K), lambda i, *_: (i, 0, 0), memory_space=pltpu.SMEM),
            pl.BlockSpec(memory_space=pl.ANY),
            pl.BlockSpec(memory_space=pl.ANY),
        ],
        out_specs=pl.BlockSpec(memory_space=pl.ANY),
        scratch_shapes=[pltpu.SemaphoreType.DMA((3,))],
    )
    return pl.pallas_call(
        functools.partial(_scatter_kernel, tc=tc, n_p=n_p, tme=tme),
        grid_spec=grid_spec,
        out_shape=jax.ShapeDtypeStruct((n_rows, half), jnp.uint32),
        compiler_params=pltpu.CompilerParams(dimension_semantics=("arbitrary",)),
        name="scatter_rows",
        interpret=interpret,
    )(counts, pad_starts, pad_ends, dest3, h2p, h2s)


def _moe_kernel(te_ref, tv_ref, ti_ref, xs_ref, w1g_ref, w1l_ref, b1g_ref, b1l_ref, w2_ref, b2_ref,
                o_ref, xb_ref, *, n_split):
    m = pl.program_id(0)
    f = pl.program_id(1)
    valid = tv_ref[m] > 0
    rows = o_ref.shape[0] // n_split

    def compute(first):
        if first:
            xb_ref[...] = _unpack_halves(xs_ref[...])
        wg = w1g_ref[0].astype(BF16)
        wl = w1l_ref[0].astype(BF16)
        w2 = w2_ref[0].astype(BF16)
        for h in range(n_split):
            sl = slice(h * rows, (h + 1) * rows)
            x = xb_ref[sl, :]
            hg = jnp.dot(x, wg, preferred_element_type=F32) + b1g_ref[0]
            hl = jnp.dot(x, wl, preferred_element_type=F32) + b1l_ref[0]
            gate = jnp.minimum(hg, SWIGLU_LIMIT)
            lin = jnp.clip(hl, -SWIGLU_LIMIT, SWIGLU_LIMIT)
            act = gate * jax.nn.sigmoid(SWIGLU_ALPHA * gate) * (lin + 1.0)
            contrib = jnp.dot(act.astype(BF16), w2, preferred_element_type=F32)
            if first:
                o_ref[sl, :] = contrib + b2_ref[0]
            else:
                o_ref[sl, :] += contrib

    @pl.when(jnp.logical_and(valid, f == 0))
    def _():
        compute(True)

    @pl.when(jnp.logical_and(valid, f > 0))
    def _():
        compute(False)

    @pl.when(jnp.logical_and(jnp.logical_not(valid), f == 0))
    def _():
        o_ref[...] = jnp.zeros_like(o_ref)


def _moe(tile_expert, tile_valid, tile_index, xs, w1, b1, w2, b2, *, tme, tf, n_split, interpret):
    n_rows, half = xs.shape
    d = 2 * half
    n_exp, _, ff2 = w1.shape
    ff = ff2 // 2
    n_f = ff // tf
    n_tiles = n_rows // tme
    b1r = b1.reshape(n_exp, 1, ff2)
    b2r = b2.reshape(n_exp, 1, d)
    grid_spec = pltpu.PrefetchScalarGridSpec(
        num_scalar_prefetch=3,
        grid=(n_tiles, n_f),
        in_specs=[
            pl.BlockSpec((tme, half), lambda m, f, te, tv, ti: (ti[m], 0)),
            pl.BlockSpec((1, d, tf), lambda m, f, te, tv, ti: (te[m], 0, f)),
            pl.BlockSpec((1, d, tf), lambda m, f, te, tv, ti: (te[m], 0, n_f + f)),
            pl.BlockSpec((1, 1, tf), lambda m, f, te, tv, ti: (te[m], 0, f)),
            pl.BlockSpec((1, 1, tf), lambda m, f, te, tv, ti: (te[m], 0, n_f + f)),
            pl.BlockSpec((1, tf, d), lambda m, f, te, tv, ti: (te[m], f, 0)),
            pl.BlockSpec((1, 1, d), lambda m, f, te, tv, ti: (te[m], 0, 0)),
        ],
        out_specs=pl.BlockSpec((tme, d), lambda m, f, te, tv, ti: (m, 0)),
        scratch_shapes=[pltpu.VMEM((tme, d), BF16)],
    )
    return pl.pallas_call(
        functools.partial(_moe_kernel, n_split=n_split),
        grid_spec=grid_spec,
        out_shape=jax.ShapeDtypeStruct((n_rows, d), F32),
        compiler_params=pltpu.CompilerParams(
            dimension_semantics=("arbitrary", "arbitrary"), vmem_limit_bytes=VMEM_LIMIT),
        name="moe",
        interpret=interpret,
    )(tile_expert, tile_valid, tile_index, xs, w1, w1, b1r, b1r, w2, b2r)


def _combine_kernel(dcur_ref, dnxt_ref, x1_ref, g_ref, out_hbm, y_ref, buf, sem, *, tc):
    i = pl.program_id(0)
    n = pl.num_programs(0)
    slot = lax.rem(i, 2)

    def issue(d_ref, sl):
        def body(j, c):
            for k in range(TOP_K):
                r = d_ref[0, 0, j * TOP_K + k]
                pltpu.make_async_copy(_row(out_hbm, r), buf.at[sl, k, pl.ds(j, 1), :],
                                      sem.at[sl]).start()
            return c

        lax.fori_loop(0, tc, body, 0, unroll=4)

    @pl.when(i == 0)
    def _():
        issue(dcur_ref, 0)

    @pl.when(i + 1 < n)
    def _():
        issue(dnxt_ref, 1 - slot)

    for _ in range(tc * TOP_K):
        pltpu.make_async_copy(_row(out_hbm, 0), buf.at[slot, 0, pl.ds(0, 1), :], sem.at[slot]).wait()
    acc = x1_ref[...]
    g = g_ref[...]
    for k in range(TOP_K):
        acc = acc + g[:, k:k + 1] * buf[slot, k]
    y_ref[...] = acc


def _combine(dest, x1, gates_t, out, *, tc, interpret):
    t, d = x1.shape
    n = t // tc
    dest3 = dest.reshape(n, 1, tc * TOP_K)
    smem_blk = lambda imap: pl.BlockSpec((1, 1, tc * TOP_K), imap, memory_space=pltpu.SMEM)
    return pl.pallas_call(
        functools.partial(_combine_kernel, tc=tc),
        grid=(n,),
        in_specs=[
            smem_blk(lambda i: (i, 0, 0)),
            smem_blk(lambda i: (jnp.minimum(i + 1, n - 1), 0, 0)),
            pl.BlockSpec((tc, d), lambda i: (i, 0)),
            pl.BlockSpec((tc, TOP_K), lambda i: (i, 0)),
            pl.BlockSpec(memory_space=pl.ANY),
        ],
        out_specs=pl.BlockSpec((tc, d), lambda i: (i, 0)),
        out_shape=jax.ShapeDtypeStruct((t, d), F32),
        scratch_shapes=[pltpu.VMEM((2, TOP_K, tc, d), F32), pltpu.SemaphoreType.DMA((2,))],
        compiler_params=pltpu.CompilerParams(
            dimension_semantics=("arbitrary",), vmem_limit_bytes=VMEM_LIMIT),
        name="combine",
        interpret=interpret,
    )(dest3, dest3, x1, gates_t, out)


def _rope_tables(pos):
    half = ROT_DIM // 2
    inv_freq = ROPE_THETA ** (-jnp.arange(0, ROT_DIM, 2, dtype=F32) / ROT_DIM)
    ang = pos.astype(F32)[:, None] * inv_freq[None, :]
    cos, sin = jnp.cos(ang), jnp.sin(ang)
    n = pos.shape[0]
    pad = HEAD_DIM - ROT_DIM
    cos_h = jnp.concatenate([cos, cos, jnp.ones((n, pad), F32)], axis=1)
    sin_h = jnp.concatenate([-sin, sin, jnp.zeros((n, pad), F32)], axis=1)
    reps = LANES // HEAD_DIM
    return jnp.tile(cos_h, (1, reps)), jnp.tile(sin_h, (1, reps))


def _routing(tope, n_exp, tme):
    t = tope.shape[1]
    n_assign = t * TOP_K
    e_flat = tope.T.reshape(-1)
    onehot = (e_flat[:, None] == jnp.arange(n_exp, dtype=jnp.int32)[None, :]).astype(jnp.int32)
    csum = jnp.cumsum(onehot, axis=0)
    counts = csum[-1]
    padded = (counts + tme - 1) // tme * tme
    pad_ends = jnp.cumsum(padded)
    pad_starts = pad_ends - padded
    dest = jnp.sum(onehot * (csum - 1 + pad_starts[None, :]), axis=1)
    n_tiles = (n_assign + n_exp * (tme - 1) + tme - 1) // tme
    tile_start = jnp.arange(n_tiles, dtype=jnp.int32) * tme
    tile_valid = (tile_start < pad_ends[-1]).astype(jnp.int32)
    last_valid = jnp.maximum(pad_ends[-1] // tme - 1, 0)
    tile_index = jnp.minimum(jnp.arange(n_tiles, dtype=jnp.int32), last_valid)
    tile_expert = jnp.sum((tile_index[:, None] * tme >= pad_ends[None, :]).astype(jnp.int32), axis=1)
    tile_expert = jnp.minimum(tile_expert, n_exp - 1)
    return dest, counts, pad_starts, pad_ends, tile_expert, tile_valid, tile_index, n_tiles


def _layer(xp, xs, ck, cv, sp, g1, w_in, g_q, g_k, sinks, w_pool, pool_scale, w_a, w_b, w_out,
           g2, w_router, b_router, w1, b1, w2, b2, *, tm_p, tm_merge, tme, tf, interpret):
    b, s, d = xp.shape
    db, ds, _ = xs.shape
    n_q = sinks.shape[0]
    n_kv = ck.shape[2]
    grp = n_q // n_kv
    qw, kw = n_q * HEAD_DIM, n_kv * HEAD_DIM
    pw = pool_scale.shape[0]
    n_exp = w_router.shape[1]
    split = qw + 2 * kw + pw

    w_qkvu = w_in[:, :split].astype(BF16)
    w_gates = w_in[:, split:].astype(BF16)
    g1r = g1.reshape(1, d)
    g2r = g2.reshape(1, d)
    gqk = jnp.concatenate([jnp.tile(g_q, n_q) * (HEAD_DIM ** -0.5), jnp.tile(g_k, n_kv)]).reshape(1, qw + kw)
    head_of_lane = jnp.arange(qw + kw, dtype=jnp.int32) // HEAD_DIM
    seg = (head_of_lane[:, None] == jnp.arange(LANES, dtype=jnp.int32)[None, :]).astype(BF16)
    segt = seg.T
    wpool = w_pool.astype(BF16)
    pscale = pool_scale.reshape(1, pw)

    def sink_cols(chunk):
        return jnp.repeat(sinks.reshape(n_kv, grp), chunk, axis=1).reshape(n_kv, grp * chunk, 1)

    front = functools.partial(_front, n_q=n_q, n_kv=n_kv, interpret=interpret)
    cos_p, sin_p = _rope_tables(jnp.arange(s, dtype=jnp.int32))
    att_p, pool_p, k_st_p, v_st_p, u_st_p = front(
        xp, g1r, w_qkvu, gqk, sink_cols(CHUNK), cos_p, sin_p,
        jnp.zeros((b, WINDOW, kw), F32), jnp.zeros((b, WINDOW, kw), F32),
        jnp.zeros((b, POOL_PAD, pw), F32), seg, segt, wpool, pscale,
        tm=tm_p, chunk=CHUNK, pos0=0)
    n_win = ck.shape[1]
    cos_s, sin_s = _rope_tables(PAST_LEN + jnp.arange(ds, dtype=jnp.int32))
    u0 = jnp.concatenate([jnp.zeros((db, POOL_PAD - POOL_HIST, pw), F32), sp], axis=1)
    att_s, pool_s, k_st_s, v_st_s, u_st_s = front(
        xs, g1r, w_qkvu, gqk, sink_cols(ds), cos_s, sin_s,
        ck.reshape(db, n_win, kw), cv.reshape(db, n_win, kw), u0, seg, segt, wpool, pscale,
        tm=ds, chunk=ds, pos0=PAST_LEN)

    wa, wb, wo = w_a.astype(BF16), w_b.astype(BF16), w_out.astype(BF16)
    wr = w_router.T.astype(BF16)
    br = b_router.reshape(n_exp, 1)
    merge = functools.partial(_merge, interpret=interpret)
    x1p, h2p, tope_p, gate_p = merge(
        xp.reshape(b * s, d), att_p.reshape(b * s, qw), pool_p.reshape(b * s, pw),
        g1r, w_gates, wa, wb, wo, g2r, wr, br, tm=tm_merge)
    x1s, h2s, tope_s, gate_s = merge(
        xs.reshape(db * ds, d), att_s.reshape(db * ds, qw), pool_s.reshape(db * ds, pw),
        g1r, w_gates, wa, wb, wo, g2r, wr, br, tm=db * ds)

    tope = jnp.concatenate([tope_p, tope_s], axis=1)
    dest, counts, pad_starts, pad_ends, tile_expert, tile_valid, tile_index, n_tiles = _routing(
        tope, n_exp, tme)
    tc = min(LANES, db * ds)
    rows = _scatter_rows(counts, pad_starts, pad_ends, dest, h2p, h2s, n_tiles * tme,
                         tc=tc, tme=tme, interpret=interpret)
    out = _moe(tile_expert, tile_valid, tile_index, rows, w1, b1, w2, b2,
               tme=tme, tf=tf, n_split=2, interpret=interpret)
    n_ap = b * s * TOP_K
    yp = _combine(dest[:n_ap], x1p, gate_p.T, out, tc=tc, interpret=interpret)
    ys = _combine(dest[n_ap:], x1s, gate_s.T, out, tc=tc, interpret=interpret)
    n_heads_shape = (n_win, n_kv, HEAD_DIM)
    return (yp.reshape(b, s, d), ys.reshape(db, ds, d),
            k_st_p.reshape(b, WINDOW, n_kv, HEAD_DIM), v_st_p.reshape(b, WINDOW, n_kv, HEAD_DIM), u_st_p,
            k_st_s.reshape((db,) + n_heads_shape), v_st_s.reshape((db,) + n_heads_shape), u_st_s)


def _forward(x_prompt, x_sample, cache_k, cache_v, state_pool, g_norm1, w_in, g_q, g_k, attn_sinks,
             w_pool, pool_scale, w_a, w_b, w_out, g_norm2, w_router, b_router, w1, b1, w2, b2,
             *, tm_p=256, tm_merge=256, tme=1024, tf=256, interpret=False):
    depth = w_in.shape[0]
    yp, ys = x_prompt, x_sample
    outs = [[] for _ in range(6)]
    for l in range(depth):
        yp, ys, *st = _layer(
            yp, ys, cache_k[l], cache_v[l], state_pool[l], g_norm1[l], w_in[l], g_q[l], g_k[l],
            attn_sinks[l], w_pool[l], pool_scale[l], w_a[l], w_b[l], w_out[l], g_norm2[l],
            w_router[l], b_router[l], w1[l], b1[l], w2[l], b2[l],
            tm_p=tm_p, tm_merge=tm_merge, tme=tme, tf=tf, interpret=interpret)
        for acc, val in zip(outs, st):
            acc.append(val)
    return (yp, ys) + tuple(jnp.stack(o) for o in outs)


def kernel(x_prompt, x_sample, cache_k, cache_v, state_pool, g_norm1, w_in, g_q, g_k, attn_sinks,
           w_pool, pool_scale, w_a, w_b, w_out, g_norm2, w_router, b_router, w1, b1, w2, b2):
    return _forward(x_prompt, x_sample, cache_k, cache_v, state_pool, g_norm1, w_in, g_q, g_k,
                    attn_sinks, w_pool, pool_scale, w_a, w_b, w_out, g_norm2, w_router, b_router,
                    w1, b1, w2, b2)
```

```python
import functools

import jax
import jax.numpy as jnp
import numpy as np
from jax import lax
from jax.experimental import pallas as pl
from jax.experimental.pallas import tpu as pltpu

F32 = jnp.float32
BF16 = jnp.bfloat16

CHUNK = 64
WINDOW = 128
HEAD_DIM = 64
ROT_DIM = HEAD_DIM // 4
ROPE_THETA = 500000.0
PAST_LEN = 2048
POOL_WINDOWS = (2, 4, 8, 16)
POOL_HIST = max(POOL_WINDOWS) - 1
POOL_PAD = 16
TOP_K = 4
SWIGLU_LIMIT = 7.0
SWIGLU_ALPHA = 1.702
NORM_EPS = 1e-5
NEG_INF = -1e30
LANES = 128
SUBLANES = 8

VMEM_LIMIT = 56 * 1024 * 1024


def _rms(x, g):
    ms = jnp.mean(x * x, axis=-1, keepdims=True)
    return x * lax.rsqrt(ms + NORM_EPS) * g


def _split_dot(a, b):
    hi = a.astype(BF16)
    lo = (a - hi.astype(F32)).astype(BF16)
    return (jnp.dot(hi, b, preferred_element_type=F32)
            + jnp.dot(lo, b, preferred_element_type=F32))


def _front_kernel(x_ref, g1_ref, w_ref, gqk_ref, sink_ref, cos_ref, sin_ref,
                  k0_ref, v0_ref, u0_ref, seg_ref, segt_ref, wpool_ref, pscale_ref,
                  att_ref, pool_ref, kst_ref, vst_ref, ust_ref,
                  kbuf, vbuf, ubuf, *tbufs, tm, chunk, pos0, n_q, n_kv, keys_on_sublanes):
    s = pl.program_id(1)
    qw = n_q * HEAD_DIM
    kw = n_kv * HEAD_DIM
    grp = n_q // n_kv

    @pl.when(s == 0)
    def _():
        kbuf[0:WINDOW, :] = k0_ref[0]
        vbuf[0:WINDOW, :] = v0_ref[0]
        ubuf[0:POOL_PAD, :] = u0_ref[0]
        if keys_on_sublanes:
            tbufs[0][:, 0:WINDOW] = v0_ref[0].T

    xn = _rms(x_ref[0], g1_ref[...]).astype(BF16)
    z = jnp.dot(xn, w_ref[...], preferred_element_type=F32)
    qk = z[:, :qw + kw]
    v = z[:, qw + kw:qw + 2 * kw]
    u = z[:, qw + 2 * kw:]

    ss = jnp.dot((qk * qk).astype(BF16), seg_ref[...], preferred_element_type=F32)
    r = lax.rsqrt(ss * (1.0 / HEAD_DIM) + NORM_EPS)
    qkn = qk * _split_dot(r, segt_ref[...]) * gqk_ref[...]

    cos = cos_ref[...]
    sin = sin_ref[...]
    lane = lax.broadcasted_iota(jnp.int32, (tm, LANES), 1)
    low_half = (lane % HEAD_DIM) < (ROT_DIM // 2)
    blocks = []
    for c in range((qw + kw) // LANES):
        blk = qkn[:, c * LANES:(c + 1) * LANES]
        swapped = jnp.where(low_half,
                            pltpu.roll(blk, LANES - ROT_DIM // 2, axis=1),
                            pltpu.roll(blk, ROT_DIM // 2, axis=1))
        blocks.append(blk * cos + swapped * sin)
    q_blocks = blocks[:qw // LANES]
    k_rot = jnp.concatenate(blocks[qw // LANES:], axis=1) if kw > LANES else blocks[qw // LANES]

    kbuf[WINDOW:WINDOW + tm, :] = k_rot
    vbuf[WINDOW:WINDOW + tm, :] = v
    ubuf[POOL_PAD:POOL_PAD + tm, :] = u

    tile_pos = pos0 + s * tm
    nk = WINDOW + chunk
    rows = grp * chunk

    def q_rows(c, h):
        qs = []
        for g in range(grp):
            head = h * grp + g
            blk = q_blocks[(head * HEAD_DIM) // LANES]
            off = (head * HEAD_DIM) % LANES
            qs.append(blk[c * chunk:(c + 1) * chunk, off:off + HEAD_DIM])
        return jnp.concatenate(qs, axis=0).astype(BF16)

    if keys_on_sublanes:
        vtbuf, att_t = tbufs
        vtbuf[:, WINDOW:WINDOW + tm] = v.T
        for c in range(tm // chunk):
            k_all = kbuf[c * chunk:c * chunk + nk, :].astype(BF16)
            base = (c * chunk) // LANES * LANES
            off = c * chunk - base
            vt_all = vtbuf[:, base:base + 2 * LANES][:, off:off + nk].astype(BF16)
            if pos0 < WINDOW:
                kpos = lax.broadcasted_iota(jnp.int32, (nk, rows), 0)
                valid = (kpos + (tile_pos + c * chunk - WINDOW)) >= 0
            for h in range(n_kv):
                kh = k_all[:, h * HEAD_DIM:(h + 1) * HEAD_DIM]
                vth = vt_all[h * HEAD_DIM:(h + 1) * HEAD_DIM, :]
                st = lax.dot_general(kh, q_rows(c, h), (((1,), (1,)), ((), ())),
                                     preferred_element_type=F32)
                if pos0 < WINDOW:
                    st = jnp.where(valid, st, NEG_INF)
                sink = sink_ref[h]
                m = jnp.maximum(jnp.max(st, axis=0, keepdims=True), sink)
                p = jnp.exp(st - m)
                denom = jnp.sum(p, axis=0, keepdims=True) + jnp.exp(sink - m)
                ot = jnp.dot(vth, p.astype(BF16), preferred_element_type=F32) / denom
                for g in range(grp):
                    head = h * grp + g
                    att_t[head * HEAD_DIM:(head + 1) * HEAD_DIM, c * chunk:(c + 1) * chunk] = (
                        ot[:, g * chunk:(g + 1) * chunk])
        att_ref[0] = att_t[...].T.astype(BF16)
        vtbuf[:, 0:WINDOW] = vtbuf[:, tm:tm + WINDOW]
    else:
        for c in range(tm // chunk):
            k_all = kbuf[c * chunk:c * chunk + nk, :].astype(BF16)
            v_all = vbuf[c * chunk:c * chunk + nk, :].astype(BF16)
            if pos0 < WINDOW:
                jpos = lax.broadcasted_iota(jnp.int32, (rows, nk), 1)
                valid = (jpos + (tile_pos + c * chunk - WINDOW)) >= 0
            for h in range(n_kv):
                kh = k_all[:, h * HEAD_DIM:(h + 1) * HEAD_DIM]
                vh = v_all[:, h * HEAD_DIM:(h + 1) * HEAD_DIM]
                sc = lax.dot_general(q_rows(c, h), kh, (((1,), (1,)), ((), ())),
                                     preferred_element_type=F32)
                if pos0 < WINDOW:
                    sc = jnp.where(valid, sc, NEG_INF)
                sink = sink_ref[h]
                m = jnp.maximum(jnp.max(sc, axis=-1, keepdims=True), sink)
                p = jnp.exp(sc - m)
                denom = jnp.sum(p, axis=-1, keepdims=True) + jnp.exp(sink - m)
                o = jnp.dot(p.astype(BF16), vh, preferred_element_type=F32) / denom
                for g in range(grp):
                    head = h * grp + g
                    att_ref[0, c * chunk:(c + 1) * chunk, head * HEAD_DIM:(head + 1) * HEAD_DIM] = (
                        o[g * chunk:(g + 1) * chunk].astype(BF16))

    posf = (tile_pos + lax.broadcasted_iota(jnp.int32, (tm, 1), 0)).astype(F32)
    pg = u.shape[1] // len(POOL_WINDOWS)
    for gi, w in enumerate(POOL_WINDOWS):
        cols = slice(gi * pg, (gi + 1) * pg)
        acc = ubuf[POOL_PAD:POOL_PAD + tm, cols]
        for j in range(1, w):
            acc = acc + ubuf[POOL_PAD - j:POOL_PAD - j + tm, cols]
        pooled = acc / jnp.minimum(posf + 1.0, float(w))
        zg = (pooled - u[:, cols]).astype(BF16)
        og = jnp.dot(zg, wpool_ref[gi], preferred_element_type=F32) * pscale_ref[:, cols]
        pool_ref[0, :, cols] = og.astype(BF16)

    kst_ref[0] = kbuf[tm:tm + WINDOW, :]
    vst_ref[0] = vbuf[tm:tm + WINDOW, :]
    ust_ref[0] = ubuf[tm + POOL_PAD - POOL_HIST:tm + POOL_PAD, :]
    k_tail = kbuf[tm:tm + WINDOW, :]
    v_tail = vbuf[tm:tm + WINDOW, :]
    u_tail = ubuf[tm:tm + POOL_PAD, :]
    kbuf[0:WINDOW, :] = k_tail
    vbuf[0:WINDOW, :] = v_tail
    ubuf[0:POOL_PAD, :] = u_tail


def _front(x, g1, w_qkvu, gqk, sink_cols, cos_t, sin_t, k0, v0, u0, seg, segt, wpool, pscale,
           *, tm, chunk, pos0, n_q, n_kv, interpret):
    b, s, d = x.shape
    qw, kw = n_q * HEAD_DIM, n_kv * HEAD_DIM
    pw = w_qkvu.shape[1] - qw - 2 * kw
    n_s = s // tm
    full = lambda shape: pl.BlockSpec(shape, lambda i, j: (0,) * len(shape))
    per_b = lambda shape: pl.BlockSpec(shape, lambda i, j: (i,) + (0,) * (len(shape) - 1))
    keys_on_sublanes = tm % LANES == 0 and (chunk * (n_q // n_kv)) % LANES == 0
    kern = functools.partial(_front_kernel, tm=tm, chunk=chunk, pos0=pos0, n_q=n_q, n_kv=n_kv,
                             keys_on_sublanes=keys_on_sublanes)
    tscratch = [pltpu.VMEM((kw, WINDOW + tm), F32), pltpu.VMEM((qw, tm), F32)] if keys_on_sublanes else []
    sink_cols = sink_cols[:, None, :] if keys_on_sublanes else sink_cols[:, :, None]
    return pl.pallas_call(
        kern,
        grid=(b, n_s),
        in_specs=[
            pl.BlockSpec((1, tm, d), lambda i, j: (i, j, 0)),
            full((1, d)),
            full(w_qkvu.shape),
            full((1, qw + kw)),
            full(sink_cols.shape),
            pl.BlockSpec((tm, LANES), lambda i, j: (j, 0)),
            pl.BlockSpec((tm, LANES), lambda i, j: (j, 0)),
            per_b((1, WINDOW, kw)),
            per_b((1, WINDOW, kw)),
            per_b((1, POOL_PAD, pw)),
            full(seg.shape),
            full(segt.shape),
            full(wpool.shape),
            full((1, pw)),
        ],
        out_specs=[
            pl.BlockSpec((1, tm, qw), lambda i, j: (i, j, 0)),
            pl.BlockSpec((1, tm, pw), lambda i, j: (i, j, 0)),
            per_b((1, WINDOW, kw)),
            per_b((1, WINDOW, kw)),
            per_b((1, POOL_HIST, pw)),
        ],
        out_shape=[
            jax.ShapeDtypeStruct((b, s, qw), BF16),
            jax.ShapeDtypeStruct((b, s, pw), BF16),
            jax.ShapeDtypeStruct((b, WINDOW, kw), F32),
            jax.ShapeDtypeStruct((b, WINDOW, kw), F32),
            jax.ShapeDtypeStruct((b, POOL_HIST, pw), F32),
        ],
        scratch_shapes=[
            pltpu.VMEM((WINDOW + tm, kw), F32),
            pltpu.VMEM((WINDOW + tm, kw), F32),
            pltpu.VMEM((POOL_PAD + tm, pw), F32),
        ] + tscratch,
        compiler_params=pltpu.CompilerParams(
            dimension_semantics=("arbitrary", "arbitrary"), vmem_limit_bytes=VMEM_LIMIT),
        name="front",
        interpret=interpret,
    )(x, g1, w_qkvu, gqk, sink_cols, cos_t, sin_t, k0, v0, u0, seg, segt, wpool, pscale)


def _merge_kernel(x_ref, att_ref, pool_ref, g1_ref, wg_ref, wa_ref, wb_ref, wo_ref, g2_ref,
                  wr_ref, br_ref, x1_ref, h2_ref, tope_ref, gate_ref, *, n_exp):
    x = x_ref[...]
    d = x.shape[1]
    tm = x.shape[0]
    xn = _rms(x, g1_ref[...]).astype(BF16)
    ga = jnp.dot(xn, wg_ref[:, :d], preferred_element_type=F32)
    m = jax.nn.sigmoid(ga) * jnp.dot(att_ref[...], wa_ref[...], preferred_element_type=F32)
    gb = jnp.dot(xn, wg_ref[:, d:], preferred_element_type=F32)
    m = m + jax.nn.sigmoid(gb) * jnp.dot(pool_ref[...], wb_ref[...], preferred_element_type=F32)
    x1 = x + jnp.dot(m.astype(BF16), wo_ref[...], preferred_element_type=F32)
    x1_ref[...] = x1
    h2f = _rms(x1, g2_ref[...])
    h2_ref[...] = h2f
    h2 = h2f.astype(BF16)

    logits = lax.dot_general(wr_ref[...], h2, (((1,), (1,)), ((), ())),
                             preferred_element_type=F32) + br_ref[...]
    idx = lax.broadcasted_iota(jnp.int32, (n_exp, tm), 0)
    vals, ids = [], []
    for _ in range(TOP_K):
        mk = jnp.max(logits, axis=0, keepdims=True)
        ik = jnp.min(jnp.where(logits == mk, idx, n_exp), axis=0, keepdims=True)
        vals.append(mk)
        ids.append(ik)
        logits = jnp.where(idx == ik, -jnp.inf, logits)
    es = [jnp.exp(vk - vals[0]) for vk in vals]
    tot = es[0]
    for ek in es[1:]:
        tot = tot + ek
    tope_ref[...] = jnp.concatenate(ids, axis=0)
    gate_ref[...] = jnp.concatenate([ek / tot for ek in es], axis=0)


def _merge(x, att, pool, g1, wg, wa, wb, wo, g2, wr, br, *, tm, interpret):
    t, d = x.shape
    n_exp = wr.shape[0]
    n_t = t // tm
    once = pl.Buffered(1)
    full = lambda a: pl.BlockSpec(a.shape, lambda i: (0,) * a.ndim, pipeline_mode=once)
    row = lambda w: pl.BlockSpec((tm, w), lambda i: (i, 0))
    return pl.pallas_call(
        functools.partial(_merge_kernel, n_exp=n_exp),
        grid=(n_t,),
        in_specs=[row(d), row(att.shape[1]), row(pool.shape[1]), full(g1), full(wg), full(wa),
                  full(wb), full(wo), full(g2), full(wr), full(br)],
        out_specs=[row(d), row(d),
                   pl.BlockSpec((TOP_K, tm), lambda i: (0, i)),
                   pl.BlockSpec((TOP_K, tm), lambda i: (0, i))],
        out_shape=[jax.ShapeDtypeStruct((t, d), F32), jax.ShapeDtypeStruct((t, d), F32),
                   jax.ShapeDtypeStruct((TOP_K, t), jnp.int32),
                   jax.ShapeDtypeStruct((TOP_K, t), F32)],
        compiler_params=pltpu.CompilerParams(
            dimension_semantics=("arbitrary",), vmem_limit_bytes=VMEM_LIMIT),
        name="merge",
        interpret=interpret,
    )(x, att, pool, g1, wg, wa, wb, wo, g2, wr, br)


def _row(ref, r):
    return ref.at[pl.ds(r, 1), :]


def _gather_kernel(cnt_ref, tokc_ref, tokn_ref, h_hbm, xs_ref, buf, sem):
    i = pl.program_id(0)
    n = pl.num_programs(0)
    slot = lax.rem(i, 2)

    def n_groups(idx):
        return (cnt_ref[idx] + (SUBLANES - 1)) // SUBLANES

    def issue(tok_ref, sl, ng):
        def body(g, c):
            for r in range(SUBLANES):
                j = g * SUBLANES + r
                pltpu.make_async_copy(_row(h_hbm, tok_ref[0, 0, j]), buf.at[sl, pl.ds(j, 1), :],
                                      sem.at[sl]).start()
            return c

        lax.fori_loop(0, ng, body, 0)

    def wait(sl, ng):
        def body(g, c):
            for _ in range(SUBLANES):
                pltpu.make_async_copy(_row(h_hbm, 0), buf.at[sl, pl.ds(0, 1), :], sem.at[sl]).wait()
            return c

        lax.fori_loop(0, ng, body, 0)

    @pl.when(i == 0)
    def _():
        buf[...] = jnp.zeros_like(buf)
        issue(tokc_ref, 0, n_groups(0))

    @pl.when(i + 1 < n)
    def _():
        issue(tokn_ref, 1 - slot, n_groups(i + 1))

    wait(slot, n_groups(i))
    xs_ref[...] = buf[slot].astype(BF16)


def _gather_rows(tile_rows, row_tok, h2, *, tg, interpret):
    t, d = h2.shape
    n = row_tok.shape[0] // tg
    tok3 = row_tok.reshape(n, 1, tg)
    tok_blk = lambda imap: pl.BlockSpec((1, 1, tg), imap, memory_space=pltpu.SMEM)
    grid_spec = pltpu.PrefetchScalarGridSpec(
        num_scalar_prefetch=1,
        grid=(n,),
        in_specs=[
            tok_blk(lambda i, cnt: (i, 0, 0)),
            tok_blk(lambda i, cnt: (jnp.minimum(i + 1, n - 1), 0, 0)),
            pl.BlockSpec(memory_space=pl.ANY),
        ],
        out_specs=pl.BlockSpec((tg, d), lambda i, cnt: (i, 0)),
        scratch_shapes=[pltpu.VMEM((2, tg, d), F32), pltpu.SemaphoreType.DMA((2,))],
    )
    return pl.pallas_call(
        _gather_kernel,
        grid_spec=grid_spec,
        out_shape=jax.ShapeDtypeStruct((n * tg, d), BF16),
        compiler_params=pltpu.CompilerParams(
            dimension_semantics=("arbitrary",), vmem_limit_bytes=VMEM_LIMIT),
        name="gather_rows",
        interpret=interpret,
    )(tile_rows, tok3, tok3, h2)


def _moe_kernel(te_ref, tv_ref, ti_ref, xs_ref, w1g_ref, w1l_ref, b1g_ref, b1l_ref, w2_ref, b2_ref,
                o_ref, *, n_split):
    m = pl.program_id(0)
    f = pl.program_id(1)
    valid = tv_ref[m] > 0
    rows = o_ref.shape[0] // n_split

    def compute(first):
        wg = w1g_ref[0].astype(BF16)
        wl = w1l_ref[0].astype(BF16)
        w2 = w2_ref[0].astype(BF16)
        for h in range(n_split):
            sl = slice(h * rows, (h + 1) * rows)
            x = xs_ref[sl, :]
            hg = jnp.dot(x, wg, preferred_element_type=F32) + b1g_ref[0]
            hl = jnp.dot(x, wl, preferred_element_type=F32) + b1l_ref[0]
            gate = jnp.minimum(hg, SWIGLU_LIMIT)
            lin = jnp.clip(hl, -SWIGLU_LIMIT, SWIGLU_LIMIT)
            act = gate * jax.nn.sigmoid(SWIGLU_ALPHA * gate) * (lin + 1.0)
            contrib = jnp.dot(act.astype(BF16), w2, preferred_element_type=F32)
            if first:
                o_ref[sl, :] = contrib + b2_ref[0]
            else:
                o_ref[sl, :] += contrib

    @pl.when(jnp.logical_and(valid, f == 0))
    def _():
        compute(True)

    @pl.when(jnp.logical_and(valid, f > 0))
    def _():
        compute(False)

    @pl.when(jnp.logical_and(jnp.logical_not(valid), f == 0))
    def _():
        o_ref[...] = jnp.zeros_like(o_ref)


def _moe(tile_expert, tile_valid, tile_index, xs, w1, b1, w2, b2, *, tme, tf, n_split, interpret):
    n_rows, d = xs.shape
    n_exp, _, ff2 = w1.shape
    ff = ff2 // 2
    n_f = ff // tf
    n_tiles = n_rows // tme
    assert tme % n_split == 0
    b1r = b1.reshape(n_exp, 1, ff2)
    b2r = b2.reshape(n_exp, 1, d)
    grid_spec = pltpu.PrefetchScalarGridSpec(
        num_scalar_prefetch=3,
        grid=(n_tiles, n_f),
        in_specs=[
            pl.BlockSpec((tme, d), lambda m, f, te, tv, ti: (ti[m], 0)),
            pl.BlockSpec((1, d, tf), lambda m, f, te, tv, ti: (te[m], 0, f)),
            pl.BlockSpec((1, d, tf), lambda m, f, te, tv, ti: (te[m], 0, n_f + f)),
            pl.BlockSpec((1, 1, tf), lambda m, f, te, tv, ti: (te[m], 0, f)),
            pl.BlockSpec((1, 1, tf), lambda m, f, te, tv, ti: (te[m], 0, n_f + f)),
            pl.BlockSpec((1, tf, d), lambda m, f, te, tv, ti: (te[m], f, 0)),
            pl.BlockSpec((1, 1, d), lambda m, f, te, tv, ti: (te[m], 0, 0)),
        ],
        out_specs=pl.BlockSpec((tme, d), lambda m, f, te, tv, ti: (m, 0)),
    )
    return pl.pallas_call(
        functools.partial(_moe_kernel, n_split=n_split),
        grid_spec=grid_spec,
        out_shape=jax.ShapeDtypeStruct((n_rows, d), F32),
        compiler_params=pltpu.CompilerParams(
            dimension_semantics=("arbitrary", "arbitrary"), vmem_limit_bytes=VMEM_LIMIT),
        name="moe",
        interpret=interpret,
    )(tile_expert, tile_valid, tile_index, xs, w1, w1, b1r, b1r, w2, b2r)


def _combine_kernel(dcur_ref, dnxt_ref, x1_ref, g_ref, out_hbm, y_ref, buf, sem, *, tc):
    i = pl.program_id(0)
    n = pl.num_programs(0)
    slot = lax.rem(i, 2)

    def issue(d_ref, sl):
        def body(j, c):
            for k in range(TOP_K):
                r = d_ref[0, 0, j * TOP_K + k]
                pltpu.make_async_copy(_row(out_hbm, r), buf.at[sl, k, pl.ds(j, 1), :],
                                      sem.at[sl]).start()
            return c

        lax.fori_loop(0, tc, body, 0, unroll=4)

    @pl.when(i == 0)
    def _():
        issue(dcur_ref, 0)

    @pl.when(i + 1 < n)
    def _():
        issue(dnxt_ref, 1 - slot)

    for _ in range(tc * TOP_K):
        pltpu.make_async_copy(_row(out_hbm, 0), buf.at[slot, 0, pl.ds(0, 1), :], sem.at[slot]).wait()
    acc = x1_ref[...]
    g = g_ref[...]
    for k in range(TOP_K):
        acc = acc + g[:, k:k + 1] * buf[slot, k]
    y_ref[...] = acc


def _combine(dest, x1, gates_t, out, *, tc, interpret):
    t, d = x1.shape
    n = t // tc
    dest3 = dest.reshape(n, 1, tc * TOP_K)
    smem_blk = lambda imap: pl.BlockSpec((1, 1, tc * TOP_K), imap, memory_space=pltpu.SMEM)
    return pl.pallas_call(
        functools.partial(_combine_kernel, tc=tc),
        grid=(n,),
        in_specs=[
            smem_blk(lambda i: (i, 0, 0)),
            smem_blk(lambda i: (jnp.minimum(i + 1, n - 1), 0, 0)),
            pl.BlockSpec((tc, d), lambda i: (i, 0)),
            pl.BlockSpec((tc, TOP_K), lambda i: (i, 0)),
            pl.BlockSpec(memory_space=pl.ANY),
        ],
        out_specs=pl.BlockSpec((tc, d), lambda i: (i, 0)),
        out_shape=jax.ShapeDtypeStruct((t, d), F32),
        scratch_shapes=[pltpu.VMEM((2, TOP_K, tc, d), F32), pltpu.SemaphoreType.DMA((2,))],
        compiler_params=pltpu.CompilerParams(
            dimension_semantics=("arbitrary",), vmem_limit_bytes=VMEM_LIMIT),
        name="combine",
        interpret=interpret,
    )(dest3, dest3, x1, gates_t, out)


def _rope_tables(pos):
    half = ROT_DIM // 2
    inv_freq = ROPE_THETA ** (-jnp.arange(0, ROT_DIM, 2, dtype=F32) / ROT_DIM)
    ang = pos.astype(F32)[:, None] * inv_freq[None, :]
    cos, sin = jnp.cos(ang), jnp.sin(ang)
    n = pos.shape[0]
    pad = HEAD_DIM - ROT_DIM
    cos_h = jnp.concatenate([cos, cos, jnp.ones((n, pad), F32)], axis=1)
    sin_h = jnp.concatenate([-sin, sin, jnp.zeros((n, pad), F32)], axis=1)
    reps = LANES // HEAD_DIM
    return jnp.tile(cos_h, (1, reps)), jnp.tile(sin_h, (1, reps))


def _routing(tope, n_exp, tme, tg):
    t = tope.shape[1]
    n_assign = t * TOP_K
    e_flat = tope.T.reshape(-1)
    onehot = (e_flat[:, None] == jnp.arange(n_exp, dtype=jnp.int32)[None, :]).astype(jnp.int32)
    csum = jnp.cumsum(onehot, axis=0)
    counts = csum[-1]
    padded = (counts + tme - 1) // tme * tme
    pad_ends = jnp.cumsum(padded)
    pad_starts = pad_ends - padded
    dest = jnp.sum(onehot * (csum - 1 + pad_starts[None, :]), axis=1)
    n_tiles = (n_assign + n_exp * (tme - 1) + tme - 1) // tme
    tile_start = jnp.arange(n_tiles, dtype=jnp.int32) * tme
    tile_valid = (tile_start < pad_ends[-1]).astype(jnp.int32)
    last_valid = jnp.maximum(pad_ends[-1] // tme - 1, 0)
    tile_index = jnp.minimum(jnp.arange(n_tiles, dtype=jnp.int32), last_valid)
    tile_expert = jnp.sum((tile_index[:, None] * tme >= pad_ends[None, :]).astype(jnp.int32), axis=1)
    tile_expert = jnp.minimum(tile_expert, n_exp - 1)
    g_start = jnp.arange(n_tiles * (tme // tg), dtype=jnp.int32) * tg
    used_end = pad_starts + counts
    overlap = jnp.minimum(g_start[:, None] + tg, used_end[None, :]) - jnp.maximum(
        g_start[:, None], pad_starts[None, :])
    gather_rows = jnp.sum(jnp.maximum(overlap, 0), axis=1).astype(jnp.int32)
    return dest, tile_expert, tile_valid, tile_index, gather_rows, n_tiles


def _layer(xp, xs, ck, cv, sp, g1, w_in, g_q, g_k, sinks, w_pool, pool_scale, w_a, w_b, w_out,
           g2, w_router, b_router, w1, b1, w2, b2, *, tm_p, tm_merge, tme, tf, interpret):
    b, s, d = xp.shape
    db, ds, _ = xs.shape
    n_q = sinks.shape[0]
    n_kv = ck.shape[2]
    grp = n_q // n_kv
    qw, kw = n_q * HEAD_DIM, n_kv * HEAD_DIM
    pw = pool_scale.shape[0]
    n_exp = w_router.shape[1]
    split = qw + 2 * kw + pw

    w_qkvu = w_in[:, :split].astype(BF16)
    w_gates = w_in[:, split:].astype(BF16)
    g1r = g1.reshape(1, d)
    g2r = g2.reshape(1, d)
    gqk = jnp.concatenate([jnp.tile(g_q, n_q) * (HEAD_DIM ** -0.5), jnp.tile(g_k, n_kv)]).reshape(1, qw + kw)
    head_of_lane = jnp.arange(qw + kw, dtype=jnp.int32) // HEAD_DIM
    seg = (head_of_lane[:, None] == jnp.arange(LANES, dtype=jnp.int32)[None, :]).astype(BF16)
    segt = seg.T
    wpool = w_pool.astype(BF16)
    pscale = pool_scale.reshape(1, pw)

    def sink_cols(chunk):
        return jnp.repeat(sinks.reshape(n_kv, grp), chunk, axis=1)

    front = functools.partial(_front, n_q=n_q, n_kv=n_kv, interpret=interpret)
    cos_p, sin_p = _rope_tables(jnp.arange(s, dtype=jnp.int32))
    att_p, pool_p, k_st_p, v_st_p, u_st_p = front(
        xp, g1r, w_qkvu, gqk, sink_cols(CHUNK), cos_p, sin_p,
        jnp.zeros((b, WINDOW, kw), F32), jnp.zeros((b, WINDOW, kw), F32),
        jnp.zeros((b, POOL_PAD, pw), F32), seg, segt, wpool, pscale,
        tm=tm_p, chunk=CHUNK, pos0=0)
    n_win = ck.shape[1]
    cos_s, sin_s = _rope_tables(PAST_LEN + jnp.arange(ds, dtype=jnp.int32))
    u0 = jnp.concatenate([jnp.zeros((db, POOL_PAD - POOL_HIST, pw), F32), sp], axis=1)
    att_s, pool_s, k_st_s, v_st_s, u_st_s = front(
        xs, g1r, w_qkvu, gqk, sink_cols(ds), cos_s, sin_s,
        ck.reshape(db, n_win, kw), cv.reshape(db, n_win, kw), u0, seg, segt, wpool, pscale,
        tm=ds, chunk=ds, pos0=PAST_LEN)

    wa, wb, wo = w_a.astype(BF16), w_b.astype(BF16), w_out.astype(BF16)
    wr = w_router.T.astype(BF16)
    br = b_router.reshape(n_exp, 1)
    merge = functools.partial(_merge, interpret=interpret)
    x1p, h2p, tope_p, gate_p = merge(
        xp.reshape(b * s, d), att_p.reshape(b * s, qw), pool_p.reshape(b * s, pw),
        g1r, w_gates, wa, wb, wo, g2r, wr, br, tm=tm_merge)
    x1s, h2s, tope_s, gate_s = merge(
        xs.reshape(db * ds, d), att_s.reshape(db * ds, qw), pool_s.reshape(db * ds, pw),
        g1r, w_gates, wa, wb, wo, g2r, wr, br, tm=db * ds)

    tope = jnp.concatenate([tope_p, tope_s], axis=1)
    h2 = jnp.concatenate([h2p, h2s], axis=0)
    t = h2.shape[0]
    tg = min(tme, 512)
    dest, tile_expert, tile_valid, tile_index, gather_rows, n_tiles = _routing(tope, n_exp, tme, tg)
    row_tok = jnp.zeros((n_tiles * tme,), jnp.int32).at[dest].set(
        jnp.arange(t * TOP_K, dtype=jnp.int32) // TOP_K)
    rows = _gather_rows(gather_rows, row_tok, h2, tg=tg, interpret=interpret)
    tc = min(LANES, db * ds)
    out = _moe(tile_expert, tile_valid, tile_index, rows, w1, b1, w2, b2,
               tme=tme, tf=tf, n_split=2, interpret=interpret)
    n_ap = b * s * TOP_K
    yp = _combine(dest[:n_ap], x1p, gate_p.T, out, tc=tc, interpret=interpret)
    ys = _combine(dest[n_ap:], x1s, gate_s.T, out, tc=tc, interpret=interpret)
    n_heads_shape = (n_win, n_kv, HEAD_DIM)
    return (yp.reshape(b, s, d), ys.reshape(db, ds, d),
            k_st_p.reshape(b, WINDOW, n_kv, HEAD_DIM), v_st_p.reshape(b, WINDOW, n_kv, HEAD_DIM), u_st_p,
            k_st_s.reshape((db,) + n_heads_shape), v_st_s.reshape((db,) + n_heads_shape), u_st_s)


def _forward(x_prompt, x_sample, cache_k, cache_v, state_pool, g_norm1, w_in, g_q, g_k, attn_sinks,
             w_pool, pool_scale, w_a, w_b, w_out, g_norm2, w_router, b_router, w1, b1, w2, b2,
             *, tm_p=512, tm_merge=256, tme=1024, tf=256, interpret=False):
    depth = w_in.shape[0]
    yp, ys = x_prompt, x_sample
    outs = [[] for _ in range(6)]
    for l in range(depth):
        yp, ys, *st = _layer(
            yp, ys, cache_k[l], cache_v[l], state_pool[l], g_norm1[l], w_in[l], g_q[l], g_k[l],
            attn_sinks[l], w_pool[l], pool_scale[l], w_a[l], w_b[l], w_out[l], g_norm2[l],
            w_router[l], b_router[l], w1[l], b1[l], w2[l], b2[l],
            tm_p=tm_p, tm_merge=tm_merge, tme=tme, tf=tf, interpret=interpret)
        for acc, val in zip(outs, st):
            acc.append(val)
    return (yp, ys) + tuple(jnp.stack(o) for o in outs)


def kernel(x_prompt, x_sample, cache_k, cache_v, state_pool, g_norm1, w_in, g_q, g_k, attn_sinks,
           w_pool, pool_scale, w_a, w_b, w_out, g_norm2, w_router, b_router, w1, b1, w2, b2):
    return _forward(x_prompt, x_sample, cache_k, cache_v, state_pool, g_norm1, w_in, g_q, g_k,
                    attn_sinks, w_pool, pool_scale, w_a, w_b, w_out, g_norm2, w_router, b_router,
                    w1, b1, w2, b2)
```

```python
import functools

import jax
import jax.numpy as jnp
import numpy as np
from jax import lax
from jax.experimental import pallas as pl
from jax.experimental.pallas import tpu as pltpu

F32 = jnp.float32
BF16 = jnp.bfloat16

CHUNK = 64
WINDOW = 128
HEAD_DIM = 64
ROT_DIM = HEAD_DIM // 4
ROPE_THETA = 500000.0
PAST_LEN = 2048
POOL_WINDOWS = (2, 4, 8, 16)
POOL_HIST = max(POOL_WINDOWS) - 1
POOL_PAD = 16
TOP_K = 4
SWIGLU_LIMIT = 7.0
SWIGLU_ALPHA = 1.702
NORM_EPS = 1e-5
NEG_INF = -1e30
LANES = 128
SUBLANES = 8

VMEM_LIMIT = 56 * 1024 * 1024


def _rms(x, g):
    ms = jnp.mean(x * x, axis=-1, keepdims=True)
    return x * lax.rsqrt(ms + NORM_EPS) * g


def _split_dot(a, b):
    hi = a.astype(BF16)
    lo = (a - hi.astype(F32)).astype(BF16)
    return (jnp.dot(hi, b, preferred_element_type=F32)
            + jnp.dot(lo, b, preferred_element_type=F32))


def _front_kernel(x_ref, g1_ref, w_ref, gqk_ref, sink_ref, cos_ref, sin_ref,
                  k0_ref, v0_ref, u0_ref, seg_ref, segt_ref, wpool_ref, pscale_ref,
                  att_ref, pool_ref, kst_ref, vst_ref, ust_ref,
                  kbuf, vbuf, ubuf, *tbufs, tm, chunk, pos0, n_q, n_kv, keys_on_sublanes):
    s = pl.program_id(1)
    qw = n_q * HEAD_DIM
    kw = n_kv * HEAD_DIM
    grp = n_q // n_kv

    @pl.when(s == 0)
    def _():
        kbuf[0:WINDOW, :] = k0_ref[0]
        vbuf[0:WINDOW, :] = v0_ref[0]
        ubuf[0:POOL_PAD, :] = u0_ref[0]
        if keys_on_sublanes:
            tbufs[0][:, 0:WINDOW] = v0_ref[0].T

    xn = _rms(x_ref[0], g1_ref[...]).astype(BF16)
    z = jnp.dot(xn, w_ref[...], preferred_element_type=F32)
    qk = z[:, :qw + kw]
    v = z[:, qw + kw:qw + 2 * kw]
    u = z[:, qw + 2 * kw:]

    ss = jnp.dot((qk * qk).astype(BF16), seg_ref[...], preferred_element_type=F32)
    r = lax.rsqrt(ss * (1.0 / HEAD_DIM) + NORM_EPS)
    qkn = qk * _split_dot(r, segt_ref[...]) * gqk_ref[...]

    cos = cos_ref[...]
    sin = sin_ref[...]
    lane = lax.broadcasted_iota(jnp.int32, (tm, LANES), 1)
    low_half = (lane % HEAD_DIM) < (ROT_DIM // 2)
    blocks = []
    for c in range((qw + kw) // LANES):
        blk = qkn[:, c * LANES:(c + 1) * LANES]
        swapped = jnp.where(low_half,
                            pltpu.roll(blk, LANES - ROT_DIM // 2, axis=1),
                            pltpu.roll(blk, ROT_DIM // 2, axis=1))
        blocks.append(blk * cos + swapped * sin)
    q_blocks = blocks[:qw // LANES]
    k_rot = jnp.concatenate(blocks[qw // LANES:], axis=1) if kw > LANES else blocks[qw // LANES]

    kbuf[WINDOW:WINDOW + tm, :] = k_rot
    vbuf[WINDOW:WINDOW + tm, :] = v
    ubuf[POOL_PAD:POOL_PAD + tm, :] = u

    tile_pos = pos0 + s * tm
    nk = WINDOW + chunk
    rows = grp * chunk

    def q_rows(c, h):
        qs = []
        for g in range(grp):
            head = h * grp + g
            blk = q_blocks[(head * HEAD_DIM) // LANES]
            off = (head * HEAD_DIM) % LANES
            qs.append(blk[c * chunk:(c + 1) * chunk, off:off + HEAD_DIM])
        return jnp.concatenate(qs, axis=0).astype(BF16)

    if keys_on_sublanes:
        vtbuf, att_t = tbufs
        vtbuf[:, WINDOW:WINDOW + tm] = v.T
        for c in range(tm // chunk):
            k_all = kbuf[c * chunk:c * chunk + nk, :].astype(BF16)
            base = (c * chunk) // LANES * LANES
            off = c * chunk - base
            vt_all = vtbuf[:, base:base + 2 * LANES][:, off:off + nk].astype(BF16)
            if pos0 < WINDOW:
                kpos = lax.broadcasted_iota(jnp.int32, (nk, rows), 0)
                valid = (kpos + (tile_pos + c * chunk - WINDOW)) >= 0
            for h in range(n_kv):
                kh = k_all[:, h * HEAD_DIM:(h + 1) * HEAD_DIM]
                vth = vt_all[h * HEAD_DIM:(h + 1) * HEAD_DIM, :]
                st = lax.dot_general(kh, q_rows(c, h), (((1,), (1,)), ((), ())),
                                     preferred_element_type=F32)
                if pos0 < WINDOW:
                    st = jnp.where(valid, st, NEG_INF)
                sink = sink_ref[h]
                m = jnp.maximum(jnp.max(st, axis=0, keepdims=True), sink)
                p = jnp.exp(st - m)
                denom = jnp.sum(p, axis=0, keepdims=True) + jnp.exp(sink - m)
                ot = jnp.dot(vth, p.astype(BF16), preferred_element_type=F32) / denom
                for g in range(grp):
                    head = h * grp + g
                    att_t[head * HEAD_DIM:(head + 1) * HEAD_DIM, c * chunk:(c + 1) * chunk] = (
                        ot[:, g * chunk:(g + 1) * chunk])
        att_ref[0] = att_t[...].T.astype(BF16)
        vtbuf[:, 0:WINDOW] = vtbuf[:, tm:tm + WINDOW]
    else:
        for c in range(tm // chunk):
            k_all = kbuf[c * chunk:c * chunk + nk, :].astype(BF16)
            v_all = vbuf[c * chunk:c * chunk + nk, :].astype(BF16)
            if pos0 < WINDOW:
                jpos = lax.broadcasted_iota(jnp.int32, (rows, nk), 1)
                valid = (jpos + (tile_pos + c * chunk - WINDOW)) >= 0
            for h in range(n_kv):
                kh = k_all[:, h * HEAD_DIM:(h + 1) * HEAD_DIM]
                vh = v_all[:, h * HEAD_DIM:(h + 1) * HEAD_DIM]
                sc = lax.dot_general(q_rows(c, h), kh, (((1,), (1,)), ((), ())),
                                     preferred_element_type=F32)
                if pos0 < WINDOW:
                    sc = jnp.where(valid, sc, NEG_INF)
                sink = sink_ref[h]
                m = jnp.maximum(jnp.max(sc, axis=-1, keepdims=True), sink)
                p = jnp.exp(sc - m)
                denom = jnp.sum(p, axis=-1, keepdims=True) + jnp.exp(sink - m)
                o = jnp.dot(p.astype(BF16), vh, preferred_element_type=F32) / denom
                for g in range(grp):
                    head = h * grp + g
                    att_ref[0, c * chunk:(c + 1) * chunk, head * HEAD_DIM:(head + 1) * HEAD_DIM] = (
                        o[g * chunk:(g + 1) * chunk].astype(BF16))

    posf = (tile_pos + lax.broadcasted_iota(jnp.int32, (tm, 1), 0)).astype(F32)
    pg = u.shape[1] // len(POOL_WINDOWS)
    for gi, w in enumerate(POOL_WINDOWS):
        cols = slice(gi * pg, (gi + 1) * pg)
        acc = ubuf[POOL_PAD:POOL_PAD + tm, cols]
        for j in range(1, w):
            acc = acc + ubuf[POOL_PAD - j:POOL_PAD - j + tm, cols]
        pooled = acc / jnp.minimum(posf + 1.0, float(w))
        zg = (pooled - u[:, cols]).astype(BF16)
        og = jnp.dot(zg, wpool_ref[gi], preferred_element_type=F32) * pscale_ref[:, cols]
        pool_ref[0, :, cols] = og.astype(BF16)

    kst_ref[0] = kbuf[tm:tm + WINDOW, :]
    vst_ref[0] = vbuf[tm:tm + WINDOW, :]
    ust_ref[0] = ubuf[tm + POOL_PAD - POOL_HIST:tm + POOL_PAD, :]
    k_tail = kbuf[tm:tm + WINDOW, :]
    v_tail = vbuf[tm:tm + WINDOW, :]
    u_tail = ubuf[tm:tm + POOL_PAD, :]
    kbuf[0:WINDOW, :] = k_tail
    vbuf[0:WINDOW, :] = v_tail
    ubuf[0:POOL_PAD, :] = u_tail


def _front(x, g1, w_qkvu, gqk, sink_cols, cos_t, sin_t, k0, v0, u0, seg, segt, wpool, pscale,
           *, tm, chunk, pos0, n_q, n_kv, interpret):
    b, s, d = x.shape
    qw, kw = n_q * HEAD_DIM, n_kv * HEAD_DIM
    pw = w_qkvu.shape[1] - qw - 2 * kw
    n_s = s // tm
    full = lambda shape: pl.BlockSpec(shape, lambda i, j: (0,) * len(shape))
    per_b = lambda shape: pl.BlockSpec(shape, lambda i, j: (i,) + (0,) * (len(shape) - 1))
    keys_on_sublanes = tm % LANES == 0 and (chunk * (n_q // n_kv)) % LANES == 0
    kern = functools.partial(_front_kernel, tm=tm, chunk=chunk, pos0=pos0, n_q=n_q, n_kv=n_kv,
                             keys_on_sublanes=keys_on_sublanes)
    tscratch = [pltpu.VMEM((kw, WINDOW + tm), F32), pltpu.VMEM((qw, tm), F32)] if keys_on_sublanes else []
    sink_cols = sink_cols[:, None, :] if keys_on_sublanes else sink_cols[:, :, None]
    return pl.pallas_call(
        kern,
        grid=(b, n_s),
        in_specs=[
            pl.BlockSpec((1, tm, d), lambda i, j: (i, j, 0)),
            full((1, d)),
            full(w_qkvu.shape),
            full((1, qw + kw)),
            full(sink_cols.shape),
            pl.BlockSpec((tm, LANES), lambda i, j: (j, 0)),
            pl.BlockSpec((tm, LANES), lambda i, j: (j, 0)),
            per_b((1, WINDOW, kw)),
            per_b((1, WINDOW, kw)),
            per_b((1, POOL_PAD, pw)),
            full(seg.shape),
            full(segt.shape),
            full(wpool.shape),
            full((1, pw)),
        ],
        out_specs=[
            pl.BlockSpec((1, tm, qw), lambda i, j: (i, j, 0)),
            pl.BlockSpec((1, tm, pw), lambda i, j: (i, j, 0)),
            per_b((1, WINDOW, kw)),
            per_b((1, WINDOW, kw)),
            per_b((1, POOL_HIST, pw)),
        ],
        out_shape=[
            jax.ShapeDtypeStruct((b, s, qw), BF16),
            jax.ShapeDtypeStruct((b, s, pw), BF16),
            jax.ShapeDtypeStruct((b, WINDOW, kw), F32),
            jax.ShapeDtypeStruct((b, WINDOW, kw), F32),
            jax.ShapeDtypeStruct((b, POOL_HIST, pw), F32),
        ],
        scratch_shapes=[
            pltpu.VMEM((WINDOW + tm, kw), F32),
            pltpu.VMEM((WINDOW + tm, kw), F32),
            pltpu.VMEM((POOL_PAD + tm, pw), F32),
        ] + tscratch,
        compiler_params=pltpu.CompilerParams(
            dimension_semantics=("arbitrary", "arbitrary"), vmem_limit_bytes=VMEM_LIMIT),
        name="front",
        interpret=interpret,
    )(x, g1, w_qkvu, gqk, sink_cols, cos_t, sin_t, k0, v0, u0, seg, segt, wpool, pscale)


def _merge_kernel(x_ref, att_ref, pool_ref, g1_ref, wg_ref, wa_ref, wb_ref, wo_ref, g2_ref,
                  wr_ref, br_ref, x1_ref, h2_ref, tope_ref, gate_ref, *, n_exp):
    x = x_ref[...]
    d = x.shape[1]
    tm = x.shape[0]
    xn = _rms(x, g1_ref[...]).astype(BF16)
    ga = jnp.dot(xn, wg_ref[:, :d], preferred_element_type=F32)
    m = jax.nn.sigmoid(ga) * jnp.dot(att_ref[...], wa_ref[...], preferred_element_type=F32)
    gb = jnp.dot(xn, wg_ref[:, d:], preferred_element_type=F32)
    m = m + jax.nn.sigmoid(gb) * jnp.dot(pool_ref[...], wb_ref[...], preferred_element_type=F32)
    x1 = x + jnp.dot(m.astype(BF16), wo_ref[...], preferred_element_type=F32)
    x1_ref[...] = x1
    h2f = _rms(x1, g2_ref[...])
    h2_ref[...] = h2f
    h2 = h2f.astype(BF16)

    logits = lax.dot_general(wr_ref[...], h2, (((1,), (1,)), ((), ())),
                             preferred_element_type=F32) + br_ref[...]
    idx = lax.broadcasted_iota(jnp.int32, (n_exp, tm), 0)
    vals, ids = [], []
    for _ in range(TOP_K):
        mk = jnp.max(logits, axis=0, keepdims=True)
        ik = jnp.min(jnp.where(logits == mk, idx, n_exp), axis=0, keepdims=True)
        vals.append(mk)
        ids.append(ik)
        logits = jnp.where(idx == ik, -jnp.inf, logits)
    es = [jnp.exp(vk - vals[0]) for vk in vals]
    tot = es[0]
    for ek in es[1:]:
        tot = tot + ek
    tope_ref[...] = jnp.concatenate(ids, axis=0)
    gate_ref[...] = jnp.concatenate([ek / tot for ek in es], axis=0)


def _merge(x, att, pool, g1, wg, wa, wb, wo, g2, wr, br, *, tm, interpret):
    t, d = x.shape
    n_exp = wr.shape[0]
    n_t = t // tm
    once = pl.Buffered(1)
    full = lambda a: pl.BlockSpec(a.shape, lambda i: (0,) * a.ndim, pipeline_mode=once)
    row = lambda w: pl.BlockSpec((tm, w), lambda i: (i, 0))
    return pl.pallas_call(
        functools.partial(_merge_kernel, n_exp=n_exp),
        grid=(n_t,),
        in_specs=[row(d), row(att.shape[1]), row(pool.shape[1]), full(g1), full(wg), full(wa),
                  full(wb), full(wo), full(g2), full(wr), full(br)],
        out_specs=[row(d), row(d),
                   pl.BlockSpec((TOP_K, tm), lambda i: (0, i)),
                   pl.BlockSpec((TOP_K, tm), lambda i: (0, i))],
        out_shape=[jax.ShapeDtypeStruct((t, d), F32), jax.ShapeDtypeStruct((t, d), F32),
                   jax.ShapeDtypeStruct((TOP_K, t), jnp.int32),
                   jax.ShapeDtypeStruct((TOP_K, t), F32)],
        compiler_params=pltpu.CompilerParams(
            dimension_semantics=("arbitrary",), vmem_limit_bytes=VMEM_LIMIT),
        name="merge",
        interpret=interpret,
    )(x, att, pool, g1, wg, wa, wb, wo, g2, wr, br)


def _row(ref, r):
    return ref.at[pl.ds(r, 1), :]


def _gather_kernel(cnt_ref, tokc_ref, tokn_ref, h_hbm, xs_ref, buf, sem):
    i = pl.program_id(0)
    n = pl.num_programs(0)
    slot = lax.rem(i, 2)

    tg = xs_ref.shape[0]

    def n_groups(idx):
        return (cnt_ref[idx] + (SUBLANES - 1)) // SUBLANES

    def fetch(tok_ref, sl, j):
        pltpu.make_async_copy(_row(h_hbm, tok_ref[0, 0, j]), buf.at[sl, pl.ds(j, 1), :],
                              sem.at[sl]).start()

    def issue(tok_ref, sl, ng):
        def body(g, c):
            for r in range(SUBLANES):
                fetch(tok_ref, sl, g * SUBLANES + r)
            return c

        lax.fori_loop(0, ng, body, 0)

    def issue_next(sl):
        ng = n_groups(i + 1)

        @pl.when(ng == tg // SUBLANES)
        def _():
            for j in range(tg):
                fetch(tokn_ref, sl, j)

        @pl.when(ng < tg // SUBLANES)
        def _():
            issue(tokn_ref, sl, ng)

    def wait(sl, ng):
        def body(g, c):
            for _ in range(SUBLANES):
                pltpu.make_async_copy(_row(h_hbm, 0), buf.at[sl, pl.ds(0, 1), :], sem.at[sl]).wait()
            return c

        lax.fori_loop(0, ng, body, 0)

    @pl.when(i == 0)
    def _():
        buf[...] = jnp.zeros_like(buf)
        issue(tokc_ref, 0, n_groups(0))

    @pl.when(jnp.logical_and(i + 1 < n, slot == 0))
    def _():
        issue_next(1)

    @pl.when(jnp.logical_and(i + 1 < n, slot == 1))
    def _():
        issue_next(0)

    wait(slot, n_groups(i))
    xs_ref[...] = buf[slot].astype(BF16)


def _gather_rows(tile_rows, row_tok, h2, *, tg, interpret):
    t, d = h2.shape
    n = row_tok.shape[0] // tg
    tok3 = row_tok.reshape(n, 1, tg)
    tok_blk = lambda imap: pl.BlockSpec((1, 1, tg), imap, memory_space=pltpu.SMEM)
    grid_spec = pltpu.PrefetchScalarGridSpec(
        num_scalar_prefetch=1,
        grid=(n,),
        in_specs=[
            tok_blk(lambda i, cnt: (i, 0, 0)),
            tok_blk(lambda i, cnt: (jnp.minimum(i + 1, n - 1), 0, 0)),
            pl.BlockSpec(memory_space=pl.ANY),
        ],
        out_specs=pl.BlockSpec((tg, d), lambda i, cnt: (i, 0)),
        scratch_shapes=[pltpu.VMEM((2, tg, d), F32), pltpu.SemaphoreType.DMA((2,))],
    )
    return pl.pallas_call(
        _gather_kernel,
        grid_spec=grid_spec,
        out_shape=jax.ShapeDtypeStruct((n * tg, d), BF16),
        compiler_params=pltpu.CompilerParams(
            dimension_semantics=("arbitrary",), vmem_limit_bytes=VMEM_LIMIT),
        name="gather_rows",
        interpret=interpret,
    )(tile_rows, tok3, tok3, h2)


def _moe_kernel(te_ref, tv_ref, ti_ref, xs_ref, w1g_ref, w1l_ref, b1g_ref, b1l_ref, w2_ref, b2_ref,
                o_ref, *, n_split):
    m = pl.program_id(0)
    f = pl.program_id(1)
    used = tv_ref[m]
    rows = o_ref.shape[0] // n_split

    def compute(first, n_groups):
        wg = w1g_ref[0].astype(BF16)
        wl = w1l_ref[0].astype(BF16)
        w2 = w2_ref[0].astype(BF16)
        if first and n_groups < n_split:
            o_ref[n_groups * rows:, :] = jnp.zeros((o_ref.shape[0] - n_groups * rows, o_ref.shape[1]), F32)
        for h in range(n_groups):
            sl = slice(h * rows, (h + 1) * rows)
            x = xs_ref[sl, :]
            hg = jnp.dot(x, wg, preferred_element_type=F32) + b1g_ref[0]
            hl = jnp.dot(x, wl, preferred_element_type=F32) + b1l_ref[0]
            gate = jnp.minimum(hg, SWIGLU_LIMIT)
            lin = jnp.clip(hl, -SWIGLU_LIMIT, SWIGLU_LIMIT)
            act = gate * jax.nn.sigmoid(SWIGLU_ALPHA * gate) * (lin + 1.0)
            contrib = jnp.dot(act.astype(BF16), w2, preferred_element_type=F32)
            if first:
                o_ref[sl, :] = contrib + b2_ref[0]
            else:
                o_ref[sl, :] += contrib

    for g in range(1, n_split + 1):
        in_range = jnp.logical_and(used > (g - 1) * rows, used <= g * rows)

        @pl.when(jnp.logical_and(in_range, f == 0))
        def _():
            compute(True, g)

        @pl.when(jnp.logical_and(in_range, f > 0))
        def _():
            compute(False, g)

    @pl.when(jnp.logical_and(used == 0, f == 0))
    def _():
        o_ref[...] = jnp.zeros_like(o_ref)


def _moe(tile_expert, tile_rows, tile_index, xs, w1, b1, w2, b2, *, tme, tf, n_split, interpret):
    n_rows, d = xs.shape
    n_exp, _, ff2 = w1.shape
    ff = ff2 // 2
    n_f = ff // tf
    n_tiles = n_rows // tme
    assert tme % n_split == 0
    b1r = b1.reshape(n_exp, 1, ff2)
    b2r = b2.reshape(n_exp, 1, d)
    grid_spec = pltpu.PrefetchScalarGridSpec(
        num_scalar_prefetch=3,
        grid=(n_tiles, n_f),
        in_specs=[
            pl.BlockSpec((tme, d), lambda m, f, te, tv, ti: (ti[m], 0)),
            pl.BlockSpec((1, d, tf), lambda m, f, te, tv, ti: (te[m], 0, f)),
            pl.BlockSpec((1, d, tf), lambda m, f, te, tv, ti: (te[m], 0, n_f + f)),
            pl.BlockSpec((1, 1, tf), lambda m, f, te, tv, ti: (te[m], 0, f)),
            pl.BlockSpec((1, 1, tf), lambda m, f, te, tv, ti: (te[m], 0, n_f + f)),
            pl.BlockSpec((1, tf, d), lambda m, f, te, tv, ti: (te[m], f, 0)),
            pl.BlockSpec((1, 1, d), lambda m, f, te, tv, ti: (te[m], 0, 0)),
        ],
        out_specs=pl.BlockSpec((tme, d), lambda m, f, te, tv, ti: (m, 0)),
    )
    return pl.pallas_call(
        functools.partial(_moe_kernel, n_split=n_split),
        grid_spec=grid_spec,
        out_shape=jax.ShapeDtypeStruct((n_rows, d), F32),
        compiler_params=pltpu.CompilerParams(
            dimension_semantics=("arbitrary", "arbitrary"), vmem_limit_bytes=VMEM_LIMIT),
        name="moe",
        interpret=interpret,
    )(tile_expert, tile_rows, tile_index, xs, w1, w1, b1r, b1r, w2, b2r)


def _combine_kernel(dcur_ref, dnxt_ref, x1_ref, g_ref, out_hbm, y_ref, buf, sem, *, tc):
    i = pl.program_id(0)
    n = pl.num_programs(0)
    slot = lax.rem(i, 2)

    def issue(d_ref, sl):
        for j in range(tc):
            for k in range(TOP_K):
                r = d_ref[0, 0, j * TOP_K + k]
                pltpu.make_async_copy(_row(out_hbm, r), buf.at[sl, k, pl.ds(j, 1), :],
                                      sem.at[sl]).start()

    @pl.when(i == 0)
    def _():
        issue(dcur_ref, 0)

    @pl.when(jnp.logical_and(i + 1 < n, slot == 0))
    def _():
        issue(dnxt_ref, 1)

    @pl.when(jnp.logical_and(i + 1 < n, slot == 1))
    def _():
        issue(dnxt_ref, 0)

    for _ in range(tc * TOP_K):
        pltpu.make_async_copy(_row(out_hbm, 0), buf.at[slot, 0, pl.ds(0, 1), :], sem.at[slot]).wait()
    acc = x1_ref[...]
    g = g_ref[...]
    for k in range(TOP_K):
        acc = acc + g[:, k:k + 1] * buf[slot, k]
    y_ref[...] = acc


def _combine(dest, x1, gates_t, out, *, tc, interpret):
    t, d = x1.shape
    n = t // tc
    dest3 = dest.reshape(n, 1, tc * TOP_K)
    smem_blk = lambda imap: pl.BlockSpec((1, 1, tc * TOP_K), imap, memory_space=pltpu.SMEM)
    return pl.pallas_call(
        functools.partial(_combine_kernel, tc=tc),
        grid=(n,),
        in_specs=[
            smem_blk(lambda i: (i, 0, 0)),
            smem_blk(lambda i: (jnp.minimum(i + 1, n - 1), 0, 0)),
            pl.BlockSpec((tc, d), lambda i: (i, 0)),
            pl.BlockSpec((tc, TOP_K), lambda i: (i, 0)),
            pl.BlockSpec(memory_space=pl.ANY),
        ],
        out_specs=pl.BlockSpec((tc, d), lambda i: (i, 0)),
        out_shape=jax.ShapeDtypeStruct((t, d), F32),
        scratch_shapes=[pltpu.VMEM((2, TOP_K, tc, d), F32), pltpu.SemaphoreType.DMA((2,))],
        compiler_params=pltpu.CompilerParams(
            dimension_semantics=("arbitrary",), vmem_limit_bytes=VMEM_LIMIT),
        name="combine",
        interpret=interpret,
    )(dest3, dest3, x1, gates_t, out)


def _rope_tables(pos):
    half = ROT_DIM // 2
    inv_freq = ROPE_THETA ** (-jnp.arange(0, ROT_DIM, 2, dtype=F32) / ROT_DIM)
    ang = pos.astype(F32)[:, None] * inv_freq[None, :]
    cos, sin = jnp.cos(ang), jnp.sin(ang)
    n = pos.shape[0]
    pad = HEAD_DIM - ROT_DIM
    cos_h = jnp.concatenate([cos, cos, jnp.ones((n, pad), F32)], axis=1)
    sin_h = jnp.concatenate([-sin, sin, jnp.zeros((n, pad), F32)], axis=1)
    reps = LANES // HEAD_DIM
    return jnp.tile(cos_h, (1, reps)), jnp.tile(sin_h, (1, reps))


def _routing(tope, n_exp, tme, tg):
    t = tope.shape[1]
    n_assign = t * TOP_K
    e_flat = tope.T.reshape(-1)
    onehot = (e_flat[:, None] == jnp.arange(n_exp, dtype=jnp.int32)[None, :]).astype(jnp.int32)
    csum = jnp.cumsum(onehot, axis=0)
    counts = csum[-1]
    padded = (counts + tme - 1) // tme * tme
    pad_ends = jnp.cumsum(padded)
    pad_starts = pad_ends - padded
    dest = jnp.sum(onehot * (csum - 1 + pad_starts[None, :]), axis=1)
    n_tiles = (n_assign + n_exp * (tme - 1) + tme - 1) // tme
    last_valid = jnp.maximum(pad_ends[-1] // tme - 1, 0)
    tile_index = jnp.minimum(jnp.arange(n_tiles, dtype=jnp.int32), last_valid)
    tile_expert = jnp.sum((tile_index[:, None] * tme >= pad_ends[None, :]).astype(jnp.int32), axis=1)
    tile_expert = jnp.minimum(tile_expert, n_exp - 1)
    g_start = jnp.arange(n_tiles * (tme // tg), dtype=jnp.int32) * tg
    used_end = pad_starts + counts
    overlap = jnp.minimum(g_start[:, None] + tg, used_end[None, :]) - jnp.maximum(
        g_start[:, None], pad_starts[None, :])
    gather_rows = jnp.sum(jnp.maximum(overlap, 0), axis=1).astype(jnp.int32)
    tile_rows = jnp.sum(gather_rows.reshape(n_tiles, tme // tg), axis=1)
    return dest, tile_expert, tile_rows, tile_index, gather_rows, n_tiles


def _layer(xp, xs, ck, cv, sp, g1, w_in, g_q, g_k, sinks, w_pool, pool_scale, w_a, w_b, w_out,
           g2, w_router, b_router, w1, b1, w2, b2, *, tm_p, tm_merge, tme, tf, interpret):
    b, s, d = xp.shape
    db, ds, _ = xs.shape
    n_q = sinks.shape[0]
    n_kv = ck.shape[2]
    grp = n_q // n_kv
    qw, kw = n_q * HEAD_DIM, n_kv * HEAD_DIM
    pw = pool_scale.shape[0]
    n_exp = w_router.shape[1]
    split = qw + 2 * kw + pw

    w_qkvu = w_in[:, :split].astype(BF16)
    w_gates = w_in[:, split:].astype(BF16)
    g1r = g1.reshape(1, d)
    g2r = g2.reshape(1, d)
    gqk = jnp.concatenate([jnp.tile(g_q, n_q) * (HEAD_DIM ** -0.5), jnp.tile(g_k, n_kv)]).reshape(1, qw + kw)
    head_of_lane = jnp.arange(qw + kw, dtype=jnp.int32) // HEAD_DIM
    seg = (head_of_lane[:, None] == jnp.arange(LANES, dtype=jnp.int32)[None, :]).astype(BF16)
    segt = seg.T
    wpool = w_pool.astype(BF16)
    pscale = pool_scale.reshape(1, pw)

    def sink_cols(chunk):
        return jnp.repeat(sinks.reshape(n_kv, grp), chunk, axis=1)

    front = functools.partial(_front, n_q=n_q, n_kv=n_kv, interpret=interpret)
    cos_p, sin_p = _rope_tables(jnp.arange(s, dtype=jnp.int32))
    att_p, pool_p, k_st_p, v_st_p, u_st_p = front(
        xp, g1r, w_qkvu, gqk, sink_cols(CHUNK), cos_p, sin_p,
        jnp.zeros((b, WINDOW, kw), F32), jnp.zeros((b, WINDOW, kw), F32),
        jnp.zeros((b, POOL_PAD, pw), F32), seg, segt, wpool, pscale,
        tm=tm_p, chunk=CHUNK, pos0=0)
    n_win = ck.shape[1]
    cos_s, sin_s = _rope_tables(PAST_LEN + jnp.arange(ds, dtype=jnp.int32))
    u0 = jnp.concatenate([jnp.zeros((db, POOL_PAD - POOL_HIST, pw), F32), sp], axis=1)
    att_s, pool_s, k_st_s, v_st_s, u_st_s = front(
        xs, g1r, w_qkvu, gqk, sink_cols(ds), cos_s, sin_s,
        ck.reshape(db, n_win, kw), cv.reshape(db, n_win, kw), u0, seg, segt, wpool, pscale,
        tm=ds, chunk=ds, pos0=PAST_LEN)

    wa, wb, wo = w_a.astype(BF16), w_b.astype(BF16), w_out.astype(BF16)
    wr = w_router.T.astype(BF16)
    br = b_router.reshape(n_exp, 1)
    merge = functools.partial(_merge, interpret=interpret)
    x1p, h2p, tope_p, gate_p = merge(
        xp.reshape(b * s, d), att_p.reshape(b * s, qw), pool_p.reshape(b * s, pw),
        g1r, w_gates, wa, wb, wo, g2r, wr, br, tm=tm_merge)
    x1s, h2s, tope_s, gate_s = merge(
        xs.reshape(db * ds, d), att_s.reshape(db * ds, qw), pool_s.reshape(db * ds, pw),
        g1r, w_gates, wa, wb, wo, g2r, wr, br, tm=db * ds)

    tope = jnp.concatenate([tope_p, tope_s], axis=1)
    h2 = jnp.concatenate([h2p, h2s], axis=0)
    t = h2.shape[0]
    tg = min(tme, 512)
    dest, tile_expert, tile_rows, tile_index, gather_rows, n_tiles = _routing(tope, n_exp, tme, tg)
    row_tok = jnp.zeros((n_tiles * tme,), jnp.int32).at[dest].set(
        jnp.arange(t * TOP_K, dtype=jnp.int32) // TOP_K)
    rows = _gather_rows(gather_rows, row_tok, h2, tg=tg, interpret=interpret)
    tc = min(LANES, db * ds)
    out = _moe(tile_expert, tile_rows, tile_index, rows, w1, b1, w2, b2,
               tme=tme, tf=tf, n_split=2, interpret=interpret)
    n_ap = b * s * TOP_K
    yp = _combine(dest[:n_ap], x1p, gate_p.T, out, tc=tc, interpret=interpret)
    ys = _combine(dest[n_ap:], x1s, gate_s.T, out, tc=tc, interpret=interpret)
    n_heads_shape = (n_win, n_kv, HEAD_DIM)
    return (yp.reshape(b, s, d), ys.reshape(db, ds, d),
            k_st_p.reshape(b, WINDOW, n_kv, HEAD_DIM), v_st_p.reshape(b, WINDOW, n_kv, HEAD_DIM), u_st_p,
            k_st_s.reshape((db,) + n_heads_shape), v_st_s.reshape((db,) + n_heads_shape), u_st_s)


def _forward(x_prompt, x_sample, cache_k, cache_v, state_pool, g_norm1, w_in, g_q, g_k, attn_sinks,
             w_pool, pool_scale, w_a, w_b, w_out, g_norm2, w_router, b_router, w1, b1, w2, b2,
             *, tm_p=512, tm_merge=256, tme=1024, tf=256, interpret=False):
    depth = w_in.shape[0]
    yp, ys = x_prompt, x_sample
    outs = [[] for _ in range(6)]
    for l in range(depth):
        yp, ys, *st = _layer(
            yp, ys, cache_k[l], cache_v[l], state_pool[l], g_norm1[l], w_in[l], g_q[l], g_k[l],
            attn_sinks[l], w_pool[l], pool_scale[l], w_a[l], w_b[l], w_out[l], g_norm2[l],
            w_router[l], b_router[l], w1[l], b1[l], w2[l], b2[l],
            tm_p=tm_p, tm_merge=tm_merge, tme=tme, tf=tf, interpret=interpret)
        for acc, val in zip(outs, st):
            acc.append(val)
    return (yp, ys) + tuple(jnp.stack(o) for o in outs)


def kernel(x_prompt, x_sample, cache_k, cache_v, state_pool, g_norm1, w_in, g_q, g_k, attn_sinks,
           w_pool, pool_scale, w_a, w_b, w_out, g_norm2, w_router, b_router, w1, b1, w2, b2):
    return _forward(x_prompt, x_sample, cache_k, cache_v, state_pool, g_norm1, w_in, g_q, g_k,
                    attn_sinks, w_pool, pool_scale, w_a, w_b, w_out, g_norm2, w_router, b_router,
                    w1, b1, w2, b2)
```

```python
import functools

import jax
import jax.numpy as jnp
import numpy as np
from jax import lax
from jax.experimental import pallas as pl
from jax.experimental.pallas import tpu as pltpu

F32 = jnp.float32
BF16 = jnp.bfloat16

CHUNK = 64
WINDOW = 128
HEAD_DIM = 64
ROT_DIM = HEAD_DIM // 4
ROPE_THETA = 500000.0
PAST_LEN = 2048
POOL_WINDOWS = (2, 4, 8, 16)
POOL_HIST = max(POOL_WINDOWS) - 1
POOL_PAD = 16
TOP_K = 4
SWIGLU_LIMIT = 7.0
SWIGLU_ALPHA = 1.702
NORM_EPS = 1e-5
NEG_INF = -1e30
LANES = 128
SUBLANES = 8

VMEM_LIMIT = 58 * 1024 * 1024


def _rms(x, g):
    ms = jnp.mean(x * x, axis=-1, keepdims=True)
    return x * lax.rsqrt(ms + NORM_EPS) * g


def _split_dot(a, b):
    hi = a.astype(BF16)
    lo = (a - hi.astype(F32)).astype(BF16)
    return (jnp.dot(hi, b, preferred_element_type=F32)
            + jnp.dot(lo, b, preferred_element_type=F32))


def _front_kernel(x_ref, g1_ref, w_ref, gqk_ref, sink_ref, cos_ref, sin_ref,
                  k0_ref, v0_ref, u0_ref, seg_ref, segt_ref, wpool_ref, pscale_ref,
                  att_ref, pool_ref, kst_ref, vst_ref, ust_ref,
                  kbuf, vbuf, ubuf, *tbufs, tm, chunk, pos0, n_q, n_kv, keys_on_sublanes):
    s = pl.program_id(1)
    qw = n_q * HEAD_DIM
    kw = n_kv * HEAD_DIM
    grp = n_q // n_kv

    @pl.when(s == 0)
    def _():
        kbuf[0:WINDOW, :] = k0_ref[0]
        vbuf[0:WINDOW, :] = v0_ref[0]
        ubuf[0:POOL_PAD, :] = u0_ref[0]
        if keys_on_sublanes:
            tbufs[0][:, 0:WINDOW] = v0_ref[0].T

    xn = _rms(x_ref[0], g1_ref[...]).astype(BF16)
    z = jnp.dot(xn, w_ref[...], preferred_element_type=F32)
    qk = z[:, :qw + kw]
    v = z[:, qw + kw:qw + 2 * kw]
    u = z[:, qw + 2 * kw:]

    ss = jnp.dot((qk * qk).astype(BF16), seg_ref[...], preferred_element_type=F32)
    r = lax.rsqrt(ss * (1.0 / HEAD_DIM) + NORM_EPS)
    qkn = qk * _split_dot(r, segt_ref[...]) * gqk_ref[...]

    cos = cos_ref[...]
    sin = sin_ref[...]
    lane = lax.broadcasted_iota(jnp.int32, (tm, LANES), 1)
    low_half = (lane % HEAD_DIM) < (ROT_DIM // 2)
    blocks = []
    for c in range((qw + kw) // LANES):
        blk = qkn[:, c * LANES:(c + 1) * LANES]
        swapped = jnp.where(low_half,
                            pltpu.roll(blk, LANES - ROT_DIM // 2, axis=1),
                            pltpu.roll(blk, ROT_DIM // 2, axis=1))
        blocks.append(blk * cos + swapped * sin)
    q_blocks = blocks[:qw // LANES]
    k_rot = jnp.concatenate(blocks[qw // LANES:], axis=1) if kw > LANES else blocks[qw // LANES]

    kbuf[WINDOW:WINDOW + tm, :] = k_rot
    vbuf[WINDOW:WINDOW + tm, :] = v
    ubuf[POOL_PAD:POOL_PAD + tm, :] = u

    tile_pos = pos0 + s * tm
    nk = WINDOW + chunk
    rows = grp * chunk

    def q_rows(c, h):
        qs = []
        for g in range(grp):
            head = h * grp + g
            blk = q_blocks[(head * HEAD_DIM) // LANES]
            off = (head * HEAD_DIM) % LANES
            qs.append(blk[c * chunk:(c + 1) * chunk, off:off + HEAD_DIM])
        return jnp.concatenate(qs, axis=0).astype(BF16)

    if keys_on_sublanes:
        vtbuf, att_t = tbufs
        vtbuf[:, WINDOW:WINDOW + tm] = v.T
        for c in range(tm // chunk):
            k_all = kbuf[c * chunk:c * chunk + nk, :].astype(BF16)
            base = (c * chunk) // LANES * LANES
            off = c * chunk - base
            vt_all = vtbuf[:, base:base + 2 * LANES][:, off:off + nk].astype(BF16)
            if pos0 < WINDOW:
                kpos = lax.broadcasted_iota(jnp.int32, (nk, rows), 0)
                valid = (kpos + (tile_pos + c * chunk - WINDOW)) >= 0
            for h in range(n_kv):
                kh = k_all[:, h * HEAD_DIM:(h + 1) * HEAD_DIM]
                vth = vt_all[h * HEAD_DIM:(h + 1) * HEAD_DIM, :]
                st = lax.dot_general(kh, q_rows(c, h), (((1,), (1,)), ((), ())),
                                     preferred_element_type=F32)
                if pos0 < WINDOW:
                    st = jnp.where(valid, st, NEG_INF)
                sink = sink_ref[h]
                m = jnp.maximum(jnp.max(st, axis=0, keepdims=True), sink)
                p = jnp.exp(st - m)
                denom = jnp.sum(p, axis=0, keepdims=True) + jnp.exp(sink - m)
                ot = jnp.dot(vth, p.astype(BF16), preferred_element_type=F32) / denom
                for g in range(grp):
                    head = h * grp + g
                    att_t[head * HEAD_DIM:(head + 1) * HEAD_DIM, c * chunk:(c + 1) * chunk] = (
                        ot[:, g * chunk:(g + 1) * chunk])
        att_ref[0] = att_t[...].T.astype(BF16)
        vtbuf[:, 0:WINDOW] = vtbuf[:, tm:tm + WINDOW]
    else:
        for c in range(tm // chunk):
            k_all = kbuf[c * chunk:c * chunk + nk, :].astype(BF16)
            v_all = vbuf[c * chunk:c * chunk + nk, :].astype(BF16)
            if pos0 < WINDOW:
                jpos = lax.broadcasted_iota(jnp.int32, (rows, nk), 1)
                valid = (jpos + (tile_pos + c * chunk - WINDOW)) >= 0
            for h in range(n_kv):
                kh = k_all[:, h * HEAD_DIM:(h + 1) * HEAD_DIM]
                vh = v_all[:, h * HEAD_DIM:(h + 1) * HEAD_DIM]
                sc = lax.dot_general(q_rows(c, h), kh, (((1,), (1,)), ((), ())),
                                     preferred_element_type=F32)
                if pos0 < WINDOW:
                    sc = jnp.where(valid, sc, NEG_INF)
                sink = sink_ref[h]
                m = jnp.maximum(jnp.max(sc, axis=-1, keepdims=True), sink)
                p = jnp.exp(sc - m)
                denom = jnp.sum(p, axis=-1, keepdims=True) + jnp.exp(sink - m)
                o = jnp.dot(p.astype(BF16), vh, preferred_element_type=F32) / denom
                for g in range(grp):
                    head = h * grp + g
                    att_ref[0, c * chunk:(c + 1) * chunk, head * HEAD_DIM:(head + 1) * HEAD_DIM] = (
                        o[g * chunk:(g + 1) * chunk].astype(BF16))

    posf = (tile_pos + lax.broadcasted_iota(jnp.int32, (tm, 1), 0)).astype(F32)
    pg = u.shape[1] // len(POOL_WINDOWS)
    for gi, w in enumerate(POOL_WINDOWS):
        cols = slice(gi * pg, (gi + 1) * pg)
        acc = ubuf[POOL_PAD:POOL_PAD + tm, cols]
        for j in range(1, w):
            acc = acc + ubuf[POOL_PAD - j:POOL_PAD - j + tm, cols]
        pooled = acc / jnp.minimum(posf + 1.0, float(w))
        zg = (pooled - u[:, cols]).astype(BF16)
        og = jnp.dot(zg, wpool_ref[gi], preferred_element_type=F32) * pscale_ref[:, cols]
        pool_ref[0, :, cols] = og.astype(BF16)

    kst_ref[0] = kbuf[tm:tm + WINDOW, :]
    vst_ref[0] = vbuf[tm:tm + WINDOW, :]
    ust_ref[0] = ubuf[tm + POOL_PAD - POOL_HIST:tm + POOL_PAD, :]
    k_tail = kbuf[tm:tm + WINDOW, :]
    v_tail = vbuf[tm:tm + WINDOW, :]
    u_tail = ubuf[tm:tm + POOL_PAD, :]
    kbuf[0:WINDOW, :] = k_tail
    vbuf[0:WINDOW, :] = v_tail
    ubuf[0:POOL_PAD, :] = u_tail


def _front(x, g1, w_qkvu, gqk, sink_cols, cos_t, sin_t, k0, v0, u0, seg, segt, wpool, pscale,
           *, tm, chunk, pos0, n_q, n_kv, interpret):
    b, s, d = x.shape
    qw, kw = n_q * HEAD_DIM, n_kv * HEAD_DIM
    pw = w_qkvu.shape[1] - qw - 2 * kw
    n_s = s // tm
    full = lambda shape: pl.BlockSpec(shape, lambda i, j: (0,) * len(shape))
    per_b = lambda shape: pl.BlockSpec(shape, lambda i, j: (i,) + (0,) * (len(shape) - 1))
    keys_on_sublanes = tm % LANES == 0 and (chunk * (n_q // n_kv)) % LANES == 0
    kern = functools.partial(_front_kernel, tm=tm, chunk=chunk, pos0=pos0, n_q=n_q, n_kv=n_kv,
                             keys_on_sublanes=keys_on_sublanes)
    tscratch = [pltpu.VMEM((kw, WINDOW + tm), F32), pltpu.VMEM((qw, tm), F32)] if keys_on_sublanes else []
    sink_cols = sink_cols[:, None, :] if keys_on_sublanes else sink_cols[:, :, None]
    return pl.pallas_call(
        kern,
        grid=(b, n_s),
        in_specs=[
            pl.BlockSpec((1, tm, d), lambda i, j: (i, j, 0)),
            full((1, d)),
            full(w_qkvu.shape),
            full((1, qw + kw)),
            full(sink_cols.shape),
            pl.BlockSpec((tm, LANES), lambda i, j: (j, 0)),
            pl.BlockSpec((tm, LANES), lambda i, j: (j, 0)),
            per_b((1, WINDOW, kw)),
            per_b((1, WINDOW, kw)),
            per_b((1, POOL_PAD, pw)),
            full(seg.shape),
            full(segt.shape),
            full(wpool.shape),
            full((1, pw)),
        ],
        out_specs=[
            pl.BlockSpec((1, tm, qw), lambda i, j: (i, j, 0)),
            pl.BlockSpec((1, tm, pw), lambda i, j: (i, j, 0)),
            per_b((1, WINDOW, kw)),
            per_b((1, WINDOW, kw)),
            per_b((1, POOL_HIST, pw)),
        ],
        out_shape=[
            jax.ShapeDtypeStruct((b, s, qw), BF16),
            jax.ShapeDtypeStruct((b, s, pw), BF16),
            jax.ShapeDtypeStruct((b, WINDOW, kw), F32),
            jax.ShapeDtypeStruct((b, WINDOW, kw), F32),
            jax.ShapeDtypeStruct((b, POOL_HIST, pw), F32),
        ],
        scratch_shapes=[
            pltpu.VMEM((WINDOW + tm, kw), F32),
            pltpu.VMEM((WINDOW + tm, kw), F32),
            pltpu.VMEM((POOL_PAD + tm, pw), F32),
        ] + tscratch,
        compiler_params=pltpu.CompilerParams(
            dimension_semantics=("arbitrary", "arbitrary"), vmem_limit_bytes=VMEM_LIMIT),
        name="front",
        interpret=interpret,
    )(x, g1, w_qkvu, gqk, sink_cols, cos_t, sin_t, k0, v0, u0, seg, segt, wpool, pscale)


def _merge_kernel(x_ref, att_ref, pool_ref, g1_ref, wg_ref, wa_ref, wb_ref, wo_ref, g2_ref,
                  wr_ref, br_ref, x1_ref, h2_ref, tope_ref, gate_ref, *, n_exp):
    x = x_ref[...]
    d = x.shape[1]
    tm = x.shape[0]
    xn = _rms(x, g1_ref[...]).astype(BF16)
    ga = jnp.dot(xn, wg_ref[:, :d], preferred_element_type=F32)
    m = jax.nn.sigmoid(ga) * jnp.dot(att_ref[...], wa_ref[...], preferred_element_type=F32)
    gb = jnp.dot(xn, wg_ref[:, d:], preferred_element_type=F32)
    m = m + jax.nn.sigmoid(gb) * jnp.dot(pool_ref[...], wb_ref[...], preferred_element_type=F32)
    x1 = x + jnp.dot(m.astype(BF16), wo_ref[...], preferred_element_type=F32)
    x1_ref[...] = x1
    h2f = _rms(x1, g2_ref[...])
    h2_ref[...] = h2f
    h2 = h2f.astype(BF16)

    logits = lax.dot_general(wr_ref[...], h2, (((1,), (1,)), ((), ())),
                             preferred_element_type=F32) + br_ref[...]
    idx = lax.broadcasted_iota(jnp.int32, (n_exp, tm), 0)
    vals, ids = [], []
    for _ in range(TOP_K):
        mk = jnp.max(logits, axis=0, keepdims=True)
        ik = jnp.min(jnp.where(logits == mk, idx, n_exp), axis=0, keepdims=True)
        vals.append(mk)
        ids.append(ik)
        logits = jnp.where(idx == ik, -jnp.inf, logits)
    es = [jnp.exp(vk - vals[0]) for vk in vals]
    tot = es[0]
    for ek in es[1:]:
        tot = tot + ek
    tope_ref[...] = jnp.concatenate(ids, axis=0)
    gate_ref[...] = jnp.concatenate([ek / tot for ek in es], axis=0)


def _merge(x, att, pool, g1, wg, wa, wb, wo, g2, wr, br, *, tm, interpret):
    t, d = x.shape
    n_exp = wr.shape[0]
    n_t = t // tm
    once = pl.Buffered(1)
    full = lambda a: pl.BlockSpec(a.shape, lambda i: (0,) * a.ndim, pipeline_mode=once)
    row = lambda w: pl.BlockSpec((tm, w), lambda i: (i, 0))
    return pl.pallas_call(
        functools.partial(_merge_kernel, n_exp=n_exp),
        grid=(n_t,),
        in_specs=[row(d), row(att.shape[1]), row(pool.shape[1]), full(g1), full(wg), full(wa),
                  full(wb), full(wo), full(g2), full(wr), full(br)],
        out_specs=[row(d), row(d),
                   pl.BlockSpec((TOP_K, tm), lambda i: (0, i)),
                   pl.BlockSpec((TOP_K, tm), lambda i: (0, i))],
        out_shape=[jax.ShapeDtypeStruct((t, d), F32), jax.ShapeDtypeStruct((t, d), F32),
                   jax.ShapeDtypeStruct((TOP_K, t), jnp.int32),
                   jax.ShapeDtypeStruct((TOP_K, t), F32)],
        compiler_params=pltpu.CompilerParams(
            dimension_semantics=("arbitrary",), vmem_limit_bytes=VMEM_LIMIT),
        name="merge",
        interpret=interpret,
    )(x, att, pool, g1, wg, wa, wb, wo, g2, wr, br)


def _row(ref, r):
    return ref.at[pl.ds(r, 1), :]


def _gather_kernel(cnt_ref, tokc_ref, tokn_ref, h_hbm, xs_ref, buf, sem):
    i = pl.program_id(0)
    n = pl.num_programs(0)
    slot = lax.rem(i, 2)

    tg = xs_ref.shape[0]

    def n_groups(idx):
        return (cnt_ref[idx] + (SUBLANES - 1)) // SUBLANES

    def fetch(tok_ref, sl, j, lane):
        pltpu.make_async_copy(_row(h_hbm, tok_ref[0, 0, j]), buf.at[sl, pl.ds(j, 1), :],
                              sem.at[sl]).start(priority=lane % 2)

    def issue(tok_ref, sl, ng):
        def body(g, c):
            for r in range(SUBLANES):
                fetch(tok_ref, sl, g * SUBLANES + r, r)
            return c

        lax.fori_loop(0, ng, body, 0)

    def issue_next(sl):
        ng = n_groups(i + 1)

        @pl.when(ng == tg // SUBLANES)
        def _():
            for j in range(tg):
                fetch(tokn_ref, sl, j, j)

        @pl.when(ng < tg // SUBLANES)
        def _():
            issue(tokn_ref, sl, ng)

    def wait(sl, ng):
        def body(g, c):
            for _ in range(SUBLANES):
                pltpu.make_async_copy(_row(h_hbm, 0), buf.at[sl, pl.ds(0, 1), :], sem.at[sl]).wait()
            return c

        lax.fori_loop(0, ng, body, 0)

    @pl.when(i == 0)
    def _():
        buf[...] = jnp.zeros_like(buf)
        issue(tokc_ref, 0, n_groups(0))

    @pl.when(jnp.logical_and(i + 1 < n, slot == 0))
    def _():
        issue_next(1)

    @pl.when(jnp.logical_and(i + 1 < n, slot == 1))
    def _():
        issue_next(0)

    wait(slot, n_groups(i))
    xs_ref[...] = buf[slot].astype(BF16)


def _gather_rows(tile_rows, row_tok, h2, *, tg, interpret):
    t, d = h2.shape
    n = row_tok.shape[0] // tg
    tok3 = row_tok.reshape(n, 1, tg)
    tok_blk = lambda imap: pl.BlockSpec((1, 1, tg), imap, memory_space=pltpu.SMEM)
    grid_spec = pltpu.PrefetchScalarGridSpec(
        num_scalar_prefetch=1,
        grid=(n,),
        in_specs=[
            tok_blk(lambda i, cnt: (i, 0, 0)),
            tok_blk(lambda i, cnt: (jnp.minimum(i + 1, n - 1), 0, 0)),
            pl.BlockSpec(memory_space=pl.ANY),
        ],
        out_specs=pl.BlockSpec((tg, d), lambda i, cnt: (i, 0)),
        scratch_shapes=[pltpu.VMEM((2, tg, d), F32), pltpu.SemaphoreType.DMA((2,))],
    )
    return pl.pallas_call(
        _gather_kernel,
        grid_spec=grid_spec,
        out_shape=jax.ShapeDtypeStruct((n * tg, d), BF16),
        compiler_params=pltpu.CompilerParams(
            dimension_semantics=("arbitrary",), vmem_limit_bytes=VMEM_LIMIT),
        name="gather_rows",
        interpret=interpret,
    )(tile_rows, tok3, tok3, h2)


def _moe_kernel(te_ref, tv_ref, ti_ref, xs_ref, w1g_ref, w1l_ref, b1g_ref, b1l_ref, w2_ref, b2_ref,
                o_ref, *, n_split):
    m = pl.program_id(0)
    f = pl.program_id(1)
    used = tv_ref[m]
    rows = o_ref.shape[0] // n_split

    def compute(first, n_groups):
        wg = w1g_ref[0].astype(BF16)
        wl = w1l_ref[0].astype(BF16)
        w2 = w2_ref[0].astype(BF16)
        if first and n_groups < n_split:
            o_ref[n_groups * rows:, :] = jnp.zeros((o_ref.shape[0] - n_groups * rows, o_ref.shape[1]), F32)
        for h in range(n_groups):
            sl = slice(h * rows, (h + 1) * rows)
            x = xs_ref[sl, :]
            hg = jnp.dot(x, wg, preferred_element_type=F32) + b1g_ref[0]
            hl = jnp.dot(x, wl, preferred_element_type=F32) + b1l_ref[0]
            gate = jnp.minimum(hg, SWIGLU_LIMIT)
            lin = jnp.clip(hl, -SWIGLU_LIMIT, SWIGLU_LIMIT)
            act = gate * jax.nn.sigmoid(SWIGLU_ALPHA * gate) * (lin + 1.0)
            contrib = jnp.dot(act.astype(BF16), w2, preferred_element_type=F32)
            if first:
                o_ref[sl, :] = contrib + b2_ref[0]
            else:
                o_ref[sl, :] += contrib

    for g in range(1, n_split + 1):
        in_range = jnp.logical_and(used > (g - 1) * rows, used <= g * rows)

        @pl.when(jnp.logical_and(in_range, f == 0))
        def _():
            compute(True, g)

        @pl.when(jnp.logical_and(in_range, f > 0))
        def _():
            compute(False, g)

    @pl.when(jnp.logical_and(used == 0, f == 0))
    def _():
        o_ref[...] = jnp.zeros_like(o_ref)


def _moe(tile_expert, tile_rows, tile_index, xs, w1, b1, w2, b2, *, tme, tf, n_split, interpret):
    n_rows, d = xs.shape
    n_exp, _, ff2 = w1.shape
    ff = ff2 // 2
    n_f = ff // tf
    n_tiles = n_rows // tme
    assert tme % n_split == 0
    b1r = b1.reshape(n_exp, 1, ff2)
    b2r = b2.reshape(n_exp, 1, d)
    grid_spec = pltpu.PrefetchScalarGridSpec(
        num_scalar_prefetch=3,
        grid=(n_tiles, n_f),
        in_specs=[
            pl.BlockSpec((tme, d), lambda m, f, te, tv, ti: (ti[m], 0)),
            pl.BlockSpec((1, d, tf), lambda m, f, te, tv, ti: (te[m], 0, f)),
            pl.BlockSpec((1, d, tf), lambda m, f, te, tv, ti: (te[m], 0, n_f + f)),
            pl.BlockSpec((1, 1, tf), lambda m, f, te, tv, ti: (te[m], 0, f)),
            pl.BlockSpec((1, 1, tf), lambda m, f, te, tv, ti: (te[m], 0, n_f + f)),
            pl.BlockSpec((1, tf, d), lambda m, f, te, tv, ti: (te[m], f, 0)),
            pl.BlockSpec((1, 1, d), lambda m, f, te, tv, ti: (te[m], 0, 0)),
        ],
        out_specs=pl.BlockSpec((tme, d), lambda m, f, te, tv, ti: (m, 0)),
    )
    return pl.pallas_call(
        functools.partial(_moe_kernel, n_split=n_split),
        grid_spec=grid_spec,
        out_shape=jax.ShapeDtypeStruct((n_rows, d), F32),
        compiler_params=pltpu.CompilerParams(
            dimension_semantics=("arbitrary", "arbitrary"), vmem_limit_bytes=VMEM_LIMIT),
        name="moe",
        interpret=interpret,
    )(tile_expert, tile_rows, tile_index, xs, w1, w1, b1r, b1r, w2, b2r)


def _combine_kernel(dcur_ref, dnxt_ref, x1_ref, g_ref, out_hbm, y_ref, buf, sem, *, tc):
    i = pl.program_id(0)
    n = pl.num_programs(0)
    slot = lax.rem(i, 2)

    def issue(d_ref, sl):
        for j in range(tc):
            for k in range(TOP_K):
                r = d_ref[0, 0, j * TOP_K + k]
                pltpu.make_async_copy(_row(out_hbm, r), buf.at[sl, k, pl.ds(j, 1), :],
                                      sem.at[sl]).start(priority=k % 2)

    @pl.when(i == 0)
    def _():
        issue(dcur_ref, 0)

    @pl.when(jnp.logical_and(i + 1 < n, slot == 0))
    def _():
        issue(dnxt_ref, 1)

    @pl.when(jnp.logical_and(i + 1 < n, slot == 1))
    def _():
        issue(dnxt_ref, 0)

    for _ in range(tc * TOP_K):
        pltpu.make_async_copy(_row(out_hbm, 0), buf.at[slot, 0, pl.ds(0, 1), :], sem.at[slot]).wait()
    acc = x1_ref[...]
    g = g_ref[...]
    for k in range(TOP_K):
        acc = acc + g[:, k:k + 1] * buf[slot, k]
    y_ref[...] = acc


def _combine(dest, x1, gates_t, out, *, tc, interpret):
    t, d = x1.shape
    n = t // tc
    dest3 = dest.reshape(n, 1, tc * TOP_K)
    smem_blk = lambda imap: pl.BlockSpec((1, 1, tc * TOP_K), imap, memory_space=pltpu.SMEM)
    return pl.pallas_call(
        functools.partial(_combine_kernel, tc=tc),
        grid=(n,),
        in_specs=[
            smem_blk(lambda i: (i, 0, 0)),
            smem_blk(lambda i: (jnp.minimum(i + 1, n - 1), 0, 0)),
            pl.BlockSpec((tc, d), lambda i: (i, 0)),
            pl.BlockSpec((tc, TOP_K), lambda i: (i, 0)),
            pl.BlockSpec(memory_space=pl.ANY),
        ],
        out_specs=pl.BlockSpec((tc, d), lambda i: (i, 0)),
        out_shape=jax.ShapeDtypeStruct((t, d), F32),
        scratch_shapes=[pltpu.VMEM((2, TOP_K, tc, d), F32), pltpu.SemaphoreType.DMA((2,))],
        compiler_params=pltpu.CompilerParams(
            dimension_semantics=("arbitrary",), vmem_limit_bytes=VMEM_LIMIT),
        name="combine",
        interpret=interpret,
    )(dest3, dest3, x1, gates_t, out)


def _rope_tables(pos):
    half = ROT_DIM // 2
    inv_freq = ROPE_THETA ** (-jnp.arange(0, ROT_DIM, 2, dtype=F32) / ROT_DIM)
    ang = pos.astype(F32)[:, None] * inv_freq[None, :]
    cos, sin = jnp.cos(ang), jnp.sin(ang)
    n = pos.shape[0]
    pad = HEAD_DIM - ROT_DIM
    cos_h = jnp.concatenate([cos, cos, jnp.ones((n, pad), F32)], axis=1)
    sin_h = jnp.concatenate([-sin, sin, jnp.zeros((n, pad), F32)], axis=1)
    reps = LANES // HEAD_DIM
    return jnp.tile(cos_h, (1, reps)), jnp.tile(sin_h, (1, reps))


def _routing(tope, n_exp, tme, tg):
    t = tope.shape[1]
    n_assign = t * TOP_K
    e_flat = tope.T.reshape(-1)
    onehot = (e_flat[:, None] == jnp.arange(n_exp, dtype=jnp.int32)[None, :]).astype(jnp.int32)
    csum = jnp.cumsum(onehot, axis=0)
    counts = csum[-1]
    padded = (counts + tme - 1) // tme * tme
    pad_ends = jnp.cumsum(padded)
    pad_starts = pad_ends - padded
    dest = jnp.sum(onehot * (csum - 1 + pad_starts[None, :]), axis=1)
    n_tiles = (n_assign + n_exp * (tme - 1) + tme - 1) // tme
    last_valid = jnp.maximum(pad_ends[-1] // tme - 1, 0)
    tile_index = jnp.minimum(jnp.arange(n_tiles, dtype=jnp.int32), last_valid)
    tile_expert = jnp.sum((tile_index[:, None] * tme >= pad_ends[None, :]).astype(jnp.int32), axis=1)
    tile_expert = jnp.minimum(tile_expert, n_exp - 1)
    g_start = jnp.arange(n_tiles * (tme // tg), dtype=jnp.int32) * tg
    used_end = pad_starts + counts
    overlap = jnp.minimum(g_start[:, None] + tg, used_end[None, :]) - jnp.maximum(
        g_start[:, None], pad_starts[None, :])
    gather_rows = jnp.sum(jnp.maximum(overlap, 0), axis=1).astype(jnp.int32)
    tile_rows = jnp.sum(gather_rows.reshape(n_tiles, tme // tg), axis=1)
    return dest, tile_expert, tile_rows, tile_index, gather_rows, n_tiles


def _layer(xp, xs, ck, cv, sp, g1, w_in, g_q, g_k, sinks, w_pool, pool_scale, w_a, w_b, w_out,
           g2, w_router, b_router, w1, b1, w2, b2, *, tm_p, tm_merge, tme, tf, interpret):
    b, s, d = xp.shape
    db, ds, _ = xs.shape
    n_q = sinks.shape[0]
    n_kv = ck.shape[2]
    grp = n_q // n_kv
    qw, kw = n_q * HEAD_DIM, n_kv * HEAD_DIM
    pw = pool_scale.shape[0]
    n_exp = w_router.shape[1]
    split = qw + 2 * kw + pw

    w_qkvu = w_in[:, :split].astype(BF16)
    w_gates = w_in[:, split:].astype(BF16)
    g1r = g1.reshape(1, d)
    g2r = g2.reshape(1, d)
    gqk = jnp.concatenate([jnp.tile(g_q, n_q) * (HEAD_DIM ** -0.5), jnp.tile(g_k, n_kv)]).reshape(1, qw + kw)
    head_of_lane = jnp.arange(qw + kw, dtype=jnp.int32) // HEAD_DIM
    seg = (head_of_lane[:, None] == jnp.arange(LANES, dtype=jnp.int32)[None, :]).astype(BF16)
    segt = seg.T
    wpool = w_pool.astype(BF16)
    pscale = pool_scale.reshape(1, pw)

    def sink_cols(chunk):
        return jnp.repeat(sinks.reshape(n_kv, grp), chunk, axis=1)

    front = functools.partial(_front, n_q=n_q, n_kv=n_kv, interpret=interpret)
    cos_p, sin_p = _rope_tables(jnp.arange(s, dtype=jnp.int32))
    att_p, pool_p, k_st_p, v_st_p, u_st_p = front(
        xp, g1r, w_qkvu, gqk, sink_cols(CHUNK), cos_p, sin_p,
        jnp.zeros((b, WINDOW, kw), F32), jnp.zeros((b, WINDOW, kw), F32),
        jnp.zeros((b, POOL_PAD, pw), F32), seg, segt, wpool, pscale,
        tm=tm_p, chunk=CHUNK, pos0=0)
    n_win = ck.shape[1]
    cos_s, sin_s = _rope_tables(PAST_LEN + jnp.arange(ds, dtype=jnp.int32))
    u0 = jnp.concatenate([jnp.zeros((db, POOL_PAD - POOL_HIST, pw), F32), sp], axis=1)
    att_s, pool_s, k_st_s, v_st_s, u_st_s = front(
        xs, g1r, w_qkvu, gqk, sink_cols(ds), cos_s, sin_s,
        ck.reshape(db, n_win, kw), cv.reshape(db, n_win, kw), u0, seg, segt, wpool, pscale,
        tm=ds, chunk=ds, pos0=PAST_LEN)

    wa, wb, wo = w_a.astype(BF16), w_b.astype(BF16), w_out.astype(BF16)
    wr = w_router.T.astype(BF16)
    br = b_router.reshape(n_exp, 1)
    merge = functools.partial(_merge, interpret=interpret)
    x1p, h2p, tope_p, gate_p = merge(
        xp.reshape(b * s, d), att_p.reshape(b * s, qw), pool_p.reshape(b * s, pw),
        g1r, w_gates, wa, wb, wo, g2r, wr, br, tm=tm_merge)
    x1s, h2s, tope_s, gate_s = merge(
        xs.reshape(db * ds, d), att_s.reshape(db * ds, qw), pool_s.reshape(db * ds, pw),
        g1r, w_gates, wa, wb, wo, g2r, wr, br, tm=db * ds)

    tope = jnp.concatenate([tope_p, tope_s], axis=1)
    h2 = jnp.concatenate([h2p, h2s], axis=0)
    t = h2.shape[0]
    tg = min(tme, 512)
    dest, tile_expert, tile_rows, tile_index, gather_rows, n_tiles = _routing(tope, n_exp, tme, tg)
    row_tok = jnp.zeros((n_tiles * tme,), jnp.int32).at[dest].set(
        jnp.arange(t * TOP_K, dtype=jnp.int32) // TOP_K, unique_indices=True, mode='promise_in_bounds')
    rows = _gather_rows(gather_rows, row_tok, h2, tg=tg, interpret=interpret)
    tc = min(LANES, db * ds)
    out = _moe(tile_expert, tile_rows, tile_index, rows, w1, b1, w2, b2,
               tme=tme, tf=tf, n_split=2, interpret=interpret)
    n_ap = b * s * TOP_K
    yp = _combine(dest[:n_ap], x1p, gate_p.T, out, tc=tc, interpret=interpret)
    ys = _combine(dest[n_ap:], x1s, gate_s.T, out, tc=tc, interpret=interpret)
    n_heads_shape = (n_win, n_kv, HEAD_DIM)
    return (yp.reshape(b, s, d), ys.reshape(db, ds, d),
            k_st_p.reshape(b, WINDOW, n_kv, HEAD_DIM), v_st_p.reshape(b, WINDOW, n_kv, HEAD_DIM), u_st_p,
            k_st_s.reshape((db,) + n_heads_shape), v_st_s.reshape((db,) + n_heads_shape), u_st_s)


def _forward(x_prompt, x_sample, cache_k, cache_v, state_pool, g_norm1, w_in, g_q, g_k, attn_sinks,
             w_pool, pool_scale, w_a, w_b, w_out, g_norm2, w_router, b_router, w1, b1, w2, b2,
             *, tm_p=512, tm_merge=256, tme=1024, tf=512, interpret=False):
    depth = w_in.shape[0]
    yp, ys = x_prompt, x_sample
    outs = [[] for _ in range(6)]
    for l in range(depth):
        yp, ys, *st = _layer(
            yp, ys, cache_k[l], cache_v[l], state_pool[l], g_norm1[l], w_in[l], g_q[l], g_k[l],
            attn_sinks[l], w_pool[l], pool_scale[l], w_a[l], w_b[l], w_out[l], g_norm2[l],
            w_router[l], b_router[l], w1[l], b1[l], w2[l], b2[l],
            tm_p=tm_p, tm_merge=tm_merge, tme=tme, tf=tf, interpret=interpret)
        for acc, val in zip(outs, st):
            acc.append(val)
    return (yp, ys) + tuple(jnp.stack(o) for o in outs)


def kernel(x_prompt, x_sample, cache_k, cache_v, state_pool, g_norm1, w_in, g_q, g_k, attn_sinks,
           w_pool, pool_scale, w_a, w_b, w_out, g_norm2, w_router, b_router, w1, b1, w2, b2):
    return _forward(x_prompt, x_sample, cache_k, cache_v, state_pool, g_norm1, w_in, g_q, g_k,
                    attn_sinks, w_pool, pool_scale, w_a, w_b, w_out, g_norm2, w_router, b_router,
                    w1, b1, w2, b2)
```

```python
import functools

import jax
import jax.numpy as jnp
import numpy as np
from jax import lax
from jax.experimental import pallas as pl
from jax.experimental.pallas import tpu as pltpu

F32 = jnp.float32
BF16 = jnp.bfloat16

CHUNK = 64
WINDOW = 128
HEAD_DIM = 64
ROT_DIM = HEAD_DIM // 4
ROPE_THETA = 500000.0
PAST_LEN = 2048
POOL_WINDOWS = (2, 4, 8, 16)
POOL_HIST = max(POOL_WINDOWS) - 1
POOL_PAD = 16
TOP_K = 4
SWIGLU_LIMIT = 7.0
SWIGLU_ALPHA = 1.702
NORM_EPS = 1e-5
NEG_INF = -1e30
LANES = 128
SUBLANES = 8

VMEM_LIMIT = 58 * 1024 * 1024


def _rms(x, g):
    ms = jnp.mean(x * x, axis=-1, keepdims=True)
    return x * lax.rsqrt(ms + NORM_EPS) * g


def _split_dot(a, b):
    hi = a.astype(BF16)
    lo = (a - hi.astype(F32)).astype(BF16)
    return (jnp.dot(hi, b, preferred_element_type=F32)
            + jnp.dot(lo, b, preferred_element_type=F32))


def _front_kernel(x_ref, g1_ref, w_ref, gqk_ref, sink_ref, cos_ref, sin_ref,
                  k0_ref, v0_ref, u0_ref, seg_ref, segt_ref, wpool_ref, pscale_ref,
                  att_ref, pool_ref, kst_ref, vst_ref, ust_ref,
                  kbuf, vbuf, ubuf, *tbufs, tm, chunk, pos0, n_q, n_kv, keys_on_sublanes):
    s = pl.program_id(1)
    qw = n_q * HEAD_DIM
    kw = n_kv * HEAD_DIM
    grp = n_q // n_kv

    @pl.when(s == 0)
    def _():
        kbuf[0:WINDOW, :] = k0_ref[0]
        vbuf[0:WINDOW, :] = v0_ref[0]
        ubuf[0:POOL_PAD, :] = u0_ref[0]
        if keys_on_sublanes:
            tbufs[0][:, 0:WINDOW] = v0_ref[0].T

    xn = _rms(x_ref[0], g1_ref[...]).astype(BF16)
    z = jnp.dot(xn, w_ref[...], preferred_element_type=F32)
    qk = z[:, :qw + kw]
    v = z[:, qw + kw:qw + 2 * kw]
    u = z[:, qw + 2 * kw:]

    ss = jnp.dot((qk * qk).astype(BF16), seg_ref[...], preferred_element_type=F32)
    r = lax.rsqrt(ss * (1.0 / HEAD_DIM) + NORM_EPS)
    qkn = qk * _split_dot(r, segt_ref[...]) * gqk_ref[...]

    cos = cos_ref[...]
    sin = sin_ref[...]
    lane = lax.broadcasted_iota(jnp.int32, (tm, LANES), 1)
    low_half = (lane % HEAD_DIM) < (ROT_DIM // 2)
    blocks = []
    for c in range((qw + kw) // LANES):
        blk = qkn[:, c * LANES:(c + 1) * LANES]
        swapped = jnp.where(low_half,
                            pltpu.roll(blk, LANES - ROT_DIM // 2, axis=1),
                            pltpu.roll(blk, ROT_DIM // 2, axis=1))
        blocks.append(blk * cos + swapped * sin)
    q_blocks = blocks[:qw // LANES]
    k_rot = jnp.concatenate(blocks[qw // LANES:], axis=1) if kw > LANES else blocks[qw // LANES]

    kbuf[WINDOW:WINDOW + tm, :] = k_rot
    vbuf[WINDOW:WINDOW + tm, :] = v
    ubuf[POOL_PAD:POOL_PAD + tm, :] = u

    tile_pos = pos0 + s * tm
    nk = WINDOW + chunk
    rows = grp * chunk

    def q_rows(c, h):
        qs = []
        for g in range(grp):
            head = h * grp + g
            blk = q_blocks[(head * HEAD_DIM) // LANES]
            off = (head * HEAD_DIM) % LANES
            qs.append(blk[c * chunk:(c + 1) * chunk, off:off + HEAD_DIM])
        return jnp.concatenate(qs, axis=0).astype(BF16)

    if keys_on_sublanes:
        vtbuf, att_t = tbufs
        vtbuf[:, WINDOW:WINDOW + tm] = v.T
        for c in range(tm // chunk):
            k_all = kbuf[c * chunk:c * chunk + nk, :].astype(BF16)
            base = (c * chunk) // LANES * LANES
            off = c * chunk - base
            vt_all = vtbuf[:, base:base + 2 * LANES][:, off:off + nk].astype(BF16)
            if pos0 < WINDOW:
                kpos = lax.broadcasted_iota(jnp.int32, (nk, rows), 0)
                valid = (kpos + (tile_pos + c * chunk - WINDOW)) >= 0
            for h in range(n_kv):
                kh = k_all[:, h * HEAD_DIM:(h + 1) * HEAD_DIM]
                vth = vt_all[h * HEAD_DIM:(h + 1) * HEAD_DIM, :]
                st = lax.dot_general(kh, q_rows(c, h), (((1,), (1,)), ((), ())),
                                     preferred_element_type=F32)
                if pos0 < WINDOW:
                    st = jnp.where(valid, st, NEG_INF)
                sink = sink_ref[h]
                m = jnp.maximum(jnp.max(st, axis=0, keepdims=True), sink)
                p = jnp.exp(st - m)
                denom = jnp.sum(p, axis=0, keepdims=True) + jnp.exp(sink - m)
                ot = jnp.dot(vth, p.astype(BF16), preferred_element_type=F32) / denom
                for g in range(grp):
                    head = h * grp + g
                    att_t[head * HEAD_DIM:(head + 1) * HEAD_DIM, c * chunk:(c + 1) * chunk] = (
                        ot[:, g * chunk:(g + 1) * chunk])
        att_ref[0] = att_t[...].T.astype(BF16)
        vtbuf[:, 0:WINDOW] = vtbuf[:, tm:tm + WINDOW]
    else:
        for c in range(tm // chunk):
            k_all = kbuf[c * chunk:c * chunk + nk, :].astype(BF16)
            v_all = vbuf[c * chunk:c * chunk + nk, :].astype(BF16)
            if pos0 < WINDOW:
                jpos = lax.broadcasted_iota(jnp.int32, (rows, nk), 1)
                valid = (jpos + (tile_pos + c * chunk - WINDOW)) >= 0
            for h in range(n_kv):
                kh = k_all[:, h * HEAD_DIM:(h + 1) * HEAD_DIM]
                vh = v_all[:, h * HEAD_DIM:(h + 1) * HEAD_DIM]
                sc = lax.dot_general(q_rows(c, h), kh, (((1,), (1,)), ((), ())),
                                     preferred_element_type=F32)
                if pos0 < WINDOW:
                    sc = jnp.where(valid, sc, NEG_INF)
                sink = sink_ref[h]
                m = jnp.maximum(jnp.max(sc, axis=-1, keepdims=True), sink)
                p = jnp.exp(sc - m)
                denom = jnp.sum(p, axis=-1, keepdims=True) + jnp.exp(sink - m)
                o = jnp.dot(p.astype(BF16), vh, preferred_element_type=F32) / denom
                for g in range(grp):
                    head = h * grp + g
                    att_ref[0, c * chunk:(c + 1) * chunk, head * HEAD_DIM:(head + 1) * HEAD_DIM] = (
                        o[g * chunk:(g + 1) * chunk].astype(BF16))

    posf = (tile_pos + lax.broadcasted_iota(jnp.int32, (tm, 1), 0)).astype(F32)
    pg = u.shape[1] // len(POOL_WINDOWS)
    for gi, w in enumerate(POOL_WINDOWS):
        cols = slice(gi * pg, (gi + 1) * pg)
        acc = ubuf[POOL_PAD:POOL_PAD + tm, cols]
        for j in range(1, w):
            acc = acc + ubuf[POOL_PAD - j:POOL_PAD - j + tm, cols]
        pooled = acc / jnp.minimum(posf + 1.0, float(w))
        zg = (pooled - u[:, cols]).astype(BF16)
        og = jnp.dot(zg, wpool_ref[gi], preferred_element_type=F32) * pscale_ref[:, cols]
        pool_ref[0, :, cols] = og.astype(BF16)

    kst_ref[0] = kbuf[tm:tm + WINDOW, :]
    vst_ref[0] = vbuf[tm:tm + WINDOW, :]
    ust_ref[0] = ubuf[tm + POOL_PAD - POOL_HIST:tm + POOL_PAD, :]
    k_tail = kbuf[tm:tm + WINDOW, :]
    v_tail = vbuf[tm:tm + WINDOW, :]
    u_tail = ubuf[tm:tm + POOL_PAD, :]
    kbuf[0:WINDOW, :] = k_tail
    vbuf[0:WINDOW, :] = v_tail
    ubuf[0:POOL_PAD, :] = u_tail


def _front(x, g1, w_qkvu, gqk, sink_cols, cos_t, sin_t, k0, v0, u0, seg, segt, wpool, pscale,
           *, tm, chunk, pos0, n_q, n_kv, interpret):
    b, s, d = x.shape
    qw, kw = n_q * HEAD_DIM, n_kv * HEAD_DIM
    pw = w_qkvu.shape[1] - qw - 2 * kw
    n_s = s // tm
    full = lambda shape: pl.BlockSpec(shape, lambda i, j: (0,) * len(shape))
    per_b = lambda shape: pl.BlockSpec(shape, lambda i, j: (i,) + (0,) * (len(shape) - 1))
    keys_on_sublanes = tm % LANES == 0 and (chunk * (n_q // n_kv)) % LANES == 0
    kern = functools.partial(_front_kernel, tm=tm, chunk=chunk, pos0=pos0, n_q=n_q, n_kv=n_kv,
                             keys_on_sublanes=keys_on_sublanes)
    tscratch = [pltpu.VMEM((kw, WINDOW + tm), F32), pltpu.VMEM((qw, tm), F32)] if keys_on_sublanes else []
    sink_cols = sink_cols[:, None, :] if keys_on_sublanes else sink_cols[:, :, None]
    return pl.pallas_call(
        kern,
        grid=(b, n_s),
        in_specs=[
            pl.BlockSpec((1, tm, d), lambda i, j: (i, j, 0)),
            full((1, d)),
            full(w_qkvu.shape),
            full((1, qw + kw)),
            full(sink_cols.shape),
            pl.BlockSpec((tm, LANES), lambda i, j: (j, 0)),
            pl.BlockSpec((tm, LANES), lambda i, j: (j, 0)),
            per_b((1, WINDOW, kw)),
            per_b((1, WINDOW, kw)),
            per_b((1, POOL_PAD, pw)),
            full(seg.shape),
            full(segt.shape),
            full(wpool.shape),
            full((1, pw)),
        ],
        out_specs=[
            pl.BlockSpec((1, tm, qw), lambda i, j: (i, j, 0)),
            pl.BlockSpec((1, tm, pw), lambda i, j: (i, j, 0)),
            per_b((1, WINDOW, kw)),
            per_b((1, WINDOW, kw)),
            per_b((1, POOL_HIST, pw)),
        ],
        out_shape=[
            jax.ShapeDtypeStruct((b, s, qw), BF16),
            jax.ShapeDtypeStruct((b, s, pw), BF16),
            jax.ShapeDtypeStruct((b, WINDOW, kw), F32),
            jax.ShapeDtypeStruct((b, WINDOW, kw), F32),
            jax.ShapeDtypeStruct((b, POOL_HIST, pw), F32),
        ],
        scratch_shapes=[
            pltpu.VMEM((WINDOW + tm, kw), F32),
            pltpu.VMEM((WINDOW + tm, kw), F32),
            pltpu.VMEM((POOL_PAD + tm, pw), F32),
        ] + tscratch,
        compiler_params=pltpu.CompilerParams(
            dimension_semantics=("arbitrary", "arbitrary"), vmem_limit_bytes=VMEM_LIMIT),
        name="front",
        interpret=interpret,
    )(x, g1, w_qkvu, gqk, sink_cols, cos_t, sin_t, k0, v0, u0, seg, segt, wpool, pscale)


def _merge_kernel(x_ref, att_ref, pool_ref, g1_ref, wg_ref, wa_ref, wb_ref, wo_ref, g2_ref,
                  wr_ref, br_ref, x1_ref, h2_ref, tope_ref, gate_ref, *, n_exp, n_t):
    @pl.when(pl.program_id(0) >= n_t)
    def _():
        h2_ref[...] = jnp.zeros_like(h2_ref)

    @pl.when(pl.program_id(0) < n_t)
    def _():
        _merge_tile(x_ref, att_ref, pool_ref, g1_ref, wg_ref, wa_ref, wb_ref, wo_ref, g2_ref,
                    wr_ref, br_ref, x1_ref, h2_ref, tope_ref, gate_ref, n_exp)


def _merge_tile(x_ref, att_ref, pool_ref, g1_ref, wg_ref, wa_ref, wb_ref, wo_ref, g2_ref,
                wr_ref, br_ref, x1_ref, h2_ref, tope_ref, gate_ref, n_exp):
    x = x_ref[...]
    d = x.shape[1]
    tm = x.shape[0]
    xn = _rms(x, g1_ref[...]).astype(BF16)
    ga = jnp.dot(xn, wg_ref[:, :d], preferred_element_type=F32)
    m = jax.nn.sigmoid(ga) * jnp.dot(att_ref[...], wa_ref[...], preferred_element_type=F32)
    gb = jnp.dot(xn, wg_ref[:, d:], preferred_element_type=F32)
    m = m + jax.nn.sigmoid(gb) * jnp.dot(pool_ref[...], wb_ref[...], preferred_element_type=F32)
    x1 = x + jnp.dot(m.astype(BF16), wo_ref[...], preferred_element_type=F32)
    x1_ref[...] = x1
    h2f = _rms(x1, g2_ref[...])
    h2_ref[...] = h2f
    h2 = h2f.astype(BF16)

    logits = lax.dot_general(wr_ref[...], h2, (((1,), (1,)), ((), ())),
                             preferred_element_type=F32) + br_ref[...]
    idx = lax.broadcasted_iota(jnp.int32, (n_exp, tm), 0)
    vals, ids = [], []
    for _ in range(TOP_K):
        mk = jnp.max(logits, axis=0, keepdims=True)
        ik = jnp.min(jnp.where(logits == mk, idx, n_exp), axis=0, keepdims=True)
        vals.append(mk)
        ids.append(ik)
        logits = jnp.where(idx == ik, -jnp.inf, logits)
    es = [jnp.exp(vk - vals[0]) for vk in vals]
    tot = es[0]
    for ek in es[1:]:
        tot = tot + ek
    tope_ref[...] = jnp.concatenate(ids, axis=0)
    gate_ref[...] = jnp.concatenate([ek / tot for ek in es], axis=0)


def _merge(x, att, pool, g1, wg, wa, wb, wo, g2, wr, br, *, tm, spare_tiles, interpret):
    t, d = x.shape
    n_exp = wr.shape[0]
    n_t = t // tm
    once = pl.Buffered(1)
    full = lambda a: pl.BlockSpec(a.shape, lambda i: (0,) * a.ndim, pipeline_mode=once)
    row = lambda w: pl.BlockSpec((tm, w), lambda i: (jnp.minimum(i, n_t - 1), 0))
    col = pl.BlockSpec((TOP_K, tm), lambda i: (0, jnp.minimum(i, n_t - 1)))
    return pl.pallas_call(
        functools.partial(_merge_kernel, n_exp=n_exp, n_t=n_t),
        grid=(n_t + spare_tiles,),
        in_specs=[row(d), row(att.shape[1]), row(pool.shape[1]), full(g1), full(wg), full(wa),
                  full(wb), full(wo), full(g2), full(wr), full(br)],
        out_specs=[row(d), pl.BlockSpec((tm, d), lambda i: (i, 0)), col, col],
        out_shape=[jax.ShapeDtypeStruct((t, d), F32),
                   jax.ShapeDtypeStruct((t + spare_tiles * tm, d), F32),
                   jax.ShapeDtypeStruct((TOP_K, t), jnp.int32),
                   jax.ShapeDtypeStruct((TOP_K, t), F32)],
        compiler_params=pltpu.CompilerParams(
            dimension_semantics=("arbitrary",), vmem_limit_bytes=VMEM_LIMIT),
        name="merge",
        interpret=interpret,
    )(x, att, pool, g1, wg, wa, wb, wo, g2, wr, br)


def _row(ref, r):
    return ref.at[pl.ds(r, 1), :]


def _gather_kernel(cnt_ref, tokc_ref, tokn_ref, h_hbm, xs_ref, buf, sem):
    i = pl.program_id(0)
    n = pl.num_programs(0)
    slot = lax.rem(i, 2)

    tg = xs_ref.shape[0]

    def n_groups(idx):
        return (cnt_ref[idx] + (SUBLANES - 1)) // SUBLANES

    def fetch(tok_ref, sl, j):
        pltpu.make_async_copy(_row(h_hbm, tok_ref[0, 0, j]), buf.at[sl, pl.ds(j, 1), :],
                              sem.at[sl]).start()

    def issue(tok_ref, sl, ng):
        def body(g, c):
            for r in range(SUBLANES):
                fetch(tok_ref, sl, g * SUBLANES + r)
            return c

        lax.fori_loop(0, ng, body, 0)

    def issue_next(sl):
        ng = n_groups(i + 1)

        @pl.when(ng == tg // SUBLANES)
        def _():
            for j in range(tg):
                fetch(tokn_ref, sl, j)

        @pl.when(ng < tg // SUBLANES)
        def _():
            issue(tokn_ref, sl, ng)

    def wait(sl, ng):
        def body(g, c):
            for _ in range(SUBLANES):
                pltpu.make_async_copy(_row(h_hbm, 0), buf.at[sl, pl.ds(0, 1), :], sem.at[sl]).wait()
            return c

        lax.fori_loop(0, ng, body, 0)

    @pl.when(i == 0)
    def _():
        buf[...] = jnp.zeros_like(buf)
        issue(tokc_ref, 0, n_groups(0))

    @pl.when(jnp.logical_and(i + 1 < n, slot == 0))
    def _():
        issue_next(1)

    @pl.when(jnp.logical_and(i + 1 < n, slot == 1))
    def _():
        issue_next(0)

    wait(slot, n_groups(i))
    xs_ref[...] = buf[slot].astype(BF16)


def _gather_rows(tile_rows, row_tok, h2, *, tg, interpret):
    t, d = h2.shape
    n = row_tok.shape[0] // tg
    tok3 = row_tok.reshape(n, 1, tg)
    tok_blk = lambda imap: pl.BlockSpec((1, 1, tg), imap, memory_space=pltpu.SMEM)
    grid_spec = pltpu.PrefetchScalarGridSpec(
        num_scalar_prefetch=1,
        grid=(n,),
        in_specs=[
            tok_blk(lambda i, cnt: (i, 0, 0)),
            tok_blk(lambda i, cnt: (jnp.minimum(i + 1, n - 1), 0, 0)),
            pl.BlockSpec(memory_space=pl.ANY),
        ],
        out_specs=pl.BlockSpec((tg, d), lambda i, cnt: (i, 0)),
        scratch_shapes=[pltpu.VMEM((2, tg, d), F32), pltpu.SemaphoreType.DMA((2,))],
    )
    return pl.pallas_call(
        _gather_kernel,
        grid_spec=grid_spec,
        out_shape=jax.ShapeDtypeStruct((n * tg, d), BF16),
        compiler_params=pltpu.CompilerParams(
            dimension_semantics=("arbitrary",), vmem_limit_bytes=VMEM_LIMIT),
        name="gather_rows",
        interpret=interpret,
    )(tile_rows, tok3, tok3, h2)


def _moe_kernel(te_ref, tv_ref, ti_ref, xs_ref, w1g_ref, w1l_ref, b1g_ref, b1l_ref, w2_ref, b2_ref,
                o_ref, *, n_split):
    m = pl.program_id(0)
    f = pl.program_id(1)
    used = tv_ref[m]
    rows = o_ref.shape[0] // n_split

    def compute(first, n_groups):
        wg = w1g_ref[0].astype(BF16)
        wl = w1l_ref[0].astype(BF16)
        w2 = w2_ref[0].astype(BF16)
        if first and n_groups < n_split:
            o_ref[n_groups * rows:, :] = jnp.zeros((o_ref.shape[0] - n_groups * rows, o_ref.shape[1]), F32)
        for h in range(n_groups):
            sl = slice(h * rows, (h + 1) * rows)
            x = xs_ref[sl, :]
            hg = jnp.dot(x, wg, preferred_element_type=F32) + b1g_ref[0]
            hl = jnp.dot(x, wl, preferred_element_type=F32) + b1l_ref[0]
            gate = jnp.minimum(hg, SWIGLU_LIMIT)
            lin = jnp.clip(hl, -SWIGLU_LIMIT, SWIGLU_LIMIT)
            act = gate * jax.nn.sigmoid(SWIGLU_ALPHA * gate) * (lin + 1.0)
            contrib = jnp.dot(act.astype(BF16), w2, preferred_element_type=F32)
            if first:
                o_ref[sl, :] = contrib + b2_ref[0]
            else:
                o_ref[sl, :] += contrib

    for g in range(1, n_split + 1):
        in_range = jnp.logical_and(used > (g - 1) * rows, used <= g * rows)

        @pl.when(jnp.logical_and(in_range, f == 0))
        def _():
            compute(True, g)

        @pl.when(jnp.logical_and(in_range, f > 0))
        def _():
            compute(False, g)

    @pl.when(jnp.logical_and(used == 0, f == 0))
    def _():
        o_ref[...] = jnp.zeros_like(o_ref)


def _moe(tile_expert, tile_rows, tile_index, xs, w1, b1, w2, b2, *, tme, tf, n_split, interpret):
    n_rows, d = xs.shape
    n_exp, _, ff2 = w1.shape
    ff = ff2 // 2
    n_f = ff // tf
    n_tiles = n_rows // tme
    assert tme % n_split == 0
    b1r = b1.reshape(n_exp, 1, ff2)
    b2r = b2.reshape(n_exp, 1, d)
    grid_spec = pltpu.PrefetchScalarGridSpec(
        num_scalar_prefetch=3,
        grid=(n_tiles, n_f),
        in_specs=[
            pl.BlockSpec((tme, d), lambda m, f, te, tv, ti: (ti[m], 0)),
            pl.BlockSpec((1, d, tf), lambda m, f, te, tv, ti: (te[m], 0, f)),
            pl.BlockSpec((1, d, tf), lambda m, f, te, tv, ti: (te[m], 0, n_f + f)),
            pl.BlockSpec((1, 1, tf), lambda m, f, te, tv, ti: (te[m], 0, f)),
            pl.BlockSpec((1, 1, tf), lambda m, f, te, tv, ti: (te[m], 0, n_f + f)),
            pl.BlockSpec((1, tf, d), lambda m, f, te, tv, ti: (te[m], f, 0)),
            pl.BlockSpec((1, 1, d), lambda m, f, te, tv, ti: (te[m], 0, 0)),
        ],
        out_specs=pl.BlockSpec((tme, d), lambda m, f, te, tv, ti: (m, 0)),
    )
    return pl.pallas_call(
        functools.partial(_moe_kernel, n_split=n_split),
        grid_spec=grid_spec,
        out_shape=jax.ShapeDtypeStruct((n_rows, d), F32),
        compiler_params=pltpu.CompilerParams(
            dimension_semantics=("arbitrary", "arbitrary"), vmem_limit_bytes=VMEM_LIMIT),
        name="moe",
        interpret=interpret,
    )(tile_expert, tile_rows, tile_index, xs, w1, w1, b1r, b1r, w2, b2r)


def _combine_kernel(dcur_ref, dnxt_ref, x1_ref, g_ref, out_hbm, y_ref, buf, sem, *, tc):
    i = pl.program_id(0)
    n = pl.num_programs(0)
    slot = lax.rem(i, 2)

    def issue(d_ref, sl):
        for j in range(tc):
            for k in range(TOP_K):
                r = d_ref[0, 0, j * TOP_K + k]
                pltpu.make_async_copy(_row(out_hbm, r), buf.at[sl, k, pl.ds(j, 1), :],
                                      sem.at[sl]).start()

    @pl.when(i == 0)
    def _():
        issue(dcur_ref, 0)

    @pl.when(jnp.logical_and(i + 1 < n, slot == 0))
    def _():
        issue(dnxt_ref, 1)

    @pl.when(jnp.logical_and(i + 1 < n, slot == 1))
    def _():
        issue(dnxt_ref, 0)

    for _ in range(tc * TOP_K):
        pltpu.make_async_copy(_row(out_hbm, 0), buf.at[slot, 0, pl.ds(0, 1), :], sem.at[slot]).wait()
    acc = x1_ref[...]
    g = g_ref[...]
    for k in range(TOP_K):
        acc = acc + g[:, k:k + 1] * buf[slot, k]
    y_ref[...] = acc


def _combine(dest, x1, gates_t, out, *, tc, interpret):
    t, d = x1.shape
    n = t // tc
    dest3 = dest.reshape(n, 1, tc * TOP_K)
    smem_blk = lambda imap: pl.BlockSpec((1, 1, tc * TOP_K), imap, memory_space=pltpu.SMEM)
    return pl.pallas_call(
        functools.partial(_combine_kernel, tc=tc),
        grid=(n,),
        in_specs=[
            smem_blk(lambda i: (i, 0, 0)),
            smem_blk(lambda i: (jnp.minimum(i + 1, n - 1), 0, 0)),
            pl.BlockSpec((tc, d), lambda i: (i, 0)),
            pl.BlockSpec((tc, TOP_K), lambda i: (i, 0)),
            pl.BlockSpec(memory_space=pl.ANY),
        ],
        out_specs=pl.BlockSpec((tc, d), lambda i: (i, 0)),
        out_shape=jax.ShapeDtypeStruct((t, d), F32),
        scratch_shapes=[pltpu.VMEM((2, TOP_K, tc, d), F32), pltpu.SemaphoreType.DMA((2,))],
        compiler_params=pltpu.CompilerParams(
            dimension_semantics=("arbitrary",), vmem_limit_bytes=VMEM_LIMIT),
        name="combine",
        interpret=interpret,
    )(dest3, dest3, x1, gates_t, out)


def _rope_tables(pos):
    half = ROT_DIM // 2
    inv_freq = ROPE_THETA ** (-jnp.arange(0, ROT_DIM, 2, dtype=F32) / ROT_DIM)
    ang = pos.astype(F32)[:, None] * inv_freq[None, :]
    cos, sin = jnp.cos(ang), jnp.sin(ang)
    n = pos.shape[0]
    pad = HEAD_DIM - ROT_DIM
    cos_h = jnp.concatenate([cos, cos, jnp.ones((n, pad), F32)], axis=1)
    sin_h = jnp.concatenate([-sin, sin, jnp.zeros((n, pad), F32)], axis=1)
    reps = LANES // HEAD_DIM
    return jnp.tile(cos_h, (1, reps)), jnp.tile(sin_h, (1, reps))


def _routing(tope, n_exp, tme, tg):
    t = tope.shape[1]
    n_assign = t * TOP_K
    e_flat = tope.T.reshape(-1)
    onehot = (e_flat[:, None] == jnp.arange(n_exp, dtype=jnp.int32)[None, :]).astype(jnp.int32)
    csum = jnp.cumsum(onehot, axis=0)
    counts = csum[-1]
    padded = (counts + tme - 1) // tme * tme
    pad_ends = jnp.cumsum(padded)
    pad_starts = pad_ends - padded
    dest = jnp.sum(onehot * (csum - 1 + pad_starts[None, :]), axis=1)
    n_tiles = (n_assign + n_exp * (tme - 1) + tme - 1) // tme
    last_valid = jnp.maximum(pad_ends[-1] // tme - 1, 0)
    tile_index = jnp.minimum(jnp.arange(n_tiles, dtype=jnp.int32), last_valid)
    tile_expert = jnp.sum((tile_index[:, None] * tme >= pad_ends[None, :]).astype(jnp.int32), axis=1)
    tile_expert = jnp.minimum(tile_expert, n_exp - 1)
    g_start = jnp.arange(n_tiles * (tme // tg), dtype=jnp.int32) * tg
    used_end = pad_starts + counts
    overlap = jnp.minimum(g_start[:, None] + tg, used_end[None, :]) - jnp.maximum(
        g_start[:, None], pad_starts[None, :])
    gather_rows = jnp.sum(jnp.maximum(overlap, 0), axis=1).astype(jnp.int32)
    tile_rows = jnp.sum(gather_rows.reshape(n_tiles, tme // tg), axis=1)
    return dest, tile_expert, tile_rows, tile_index, gather_rows, n_tiles


def _layer(xp, xs, ck, cv, sp, g1, w_in, g_q, g_k, sinks, w_pool, pool_scale, w_a, w_b, w_out,
           g2, w_router, b_router, w1, b1, w2, b2, *, tm_p, tm_merge, tme, tf, interpret):
    b, s, d = xp.shape
    db, ds, _ = xs.shape
    n_q = sinks.shape[0]
    n_kv = ck.shape[2]
    grp = n_q // n_kv
    qw, kw = n_q * HEAD_DIM, n_kv * HEAD_DIM
    pw = pool_scale.shape[0]
    n_exp = w_router.shape[1]
    split = qw + 2 * kw + pw

    w_qkvu = w_in[:, :split].astype(BF16)
    w_gates = w_in[:, split:].astype(BF16)
    g1r = g1.reshape(1, d)
    g2r = g2.reshape(1, d)
    gqk = jnp.concatenate([jnp.tile(g_q, n_q) * (HEAD_DIM ** -0.5), jnp.tile(g_k, n_kv)]).reshape(1, qw + kw)
    head_of_lane = jnp.arange(qw + kw, dtype=jnp.int32) // HEAD_DIM
    seg = (head_of_lane[:, None] == jnp.arange(LANES, dtype=jnp.int32)[None, :]).astype(BF16)
    segt = seg.T
    wpool = w_pool.astype(BF16)
    pscale = pool_scale.reshape(1, pw)

    def sink_cols(chunk):
        return jnp.repeat(sinks.reshape(n_kv, grp), chunk, axis=1)

    front = functools.partial(_front, n_q=n_q, n_kv=n_kv, interpret=interpret)
    cos_p, sin_p = _rope_tables(jnp.arange(s, dtype=jnp.int32))
    att_p, pool_p, k_st_p, v_st_p, u_st_p = front(
        xp, g1r, w_qkvu, gqk, sink_cols(CHUNK), cos_p, sin_p,
        jnp.zeros((b, WINDOW, kw), F32), jnp.zeros((b, WINDOW, kw), F32),
        jnp.zeros((b, POOL_PAD, pw), F32), seg, segt, wpool, pscale,
        tm=tm_p, chunk=CHUNK, pos0=0)
    n_win = ck.shape[1]
    cos_s, sin_s = _rope_tables(PAST_LEN + jnp.arange(ds, dtype=jnp.int32))
    u0 = jnp.concatenate([jnp.zeros((db, POOL_PAD - POOL_HIST, pw), F32), sp], axis=1)
    att_s, pool_s, k_st_s, v_st_s, u_st_s = front(
        xs, g1r, w_qkvu, gqk, sink_cols(ds), cos_s, sin_s,
        ck.reshape(db, n_win, kw), cv.reshape(db, n_win, kw), u0, seg, segt, wpool, pscale,
        tm=ds, chunk=ds, pos0=PAST_LEN)

    wa, wb, wo = w_a.astype(BF16), w_b.astype(BF16), w_out.astype(BF16)
    wr = w_router.T.astype(BF16)
    br = b_router.reshape(n_exp, 1)
    merge = functools.partial(_merge, interpret=interpret)
    x1p, h2p, tope_p, gate_p = merge(
        xp.reshape(b * s, d), att_p.reshape(b * s, qw), pool_p.reshape(b * s, pw),
        g1r, w_gates, wa, wb, wo, g2r, wr, br, tm=tm_merge, spare_tiles=pl.cdiv(db * ds, tm_merge))
    x1s, h2s, tope_s, gate_s = merge(
        xs.reshape(db * ds, d), att_s.reshape(db * ds, qw), pool_s.reshape(db * ds, pw),
        g1r, w_gates, wa, wb, wo, g2r, wr, br, tm=db * ds, spare_tiles=0)

    tope = jnp.concatenate([tope_p, tope_s], axis=1)
    h2 = lax.dynamic_update_slice(h2p, h2s, (b * s, 0))
    t = b * s + db * ds
    tg = min(tme, 512)
    dest, tile_expert, tile_rows, tile_index, gather_rows, n_tiles = _routing(tope, n_exp, tme, tg)
    row_tok = jnp.zeros((n_tiles * tme,), jnp.int32).at[dest].set(
        jnp.arange(t * TOP_K, dtype=jnp.int32) // TOP_K)
    rows = _gather_rows(gather_rows, row_tok, h2, tg=tg, interpret=interpret)
    tc = min(LANES, db * ds)
    out = _moe(tile_expert, tile_rows, tile_index, rows, w1, b1, w2, b2,
               tme=tme, tf=tf, n_split=2, interpret=interpret)
    n_ap = b * s * TOP_K
    yp = _combine(dest[:n_ap], x1p, gate_p.T, out, tc=tc, interpret=interpret)
    ys = _combine(dest[n_ap:], x1s, gate_s.T, out, tc=tc, interpret=interpret)
    n_heads_shape = (n_win, n_kv, HEAD_DIM)
    return (yp.reshape(b, s, d), ys.reshape(db, ds, d),
            k_st_p.reshape(b, WINDOW, n_kv, HEAD_DIM), v_st_p.reshape(b, WINDOW, n_kv, HEAD_DIM), u_st_p,
            k_st_s.reshape((db,) + n_heads_shape), v_st_s.reshape((db,) + n_heads_shape), u_st_s)


def _forward(x_prompt, x_sample, cache_k, cache_v, state_pool, g_norm1, w_in, g_q, g_k, attn_sinks,
             w_pool, pool_scale, w_a, w_b, w_out, g_norm2, w_router, b_router, w1, b1, w2, b2,
             *, tm_p=512, tm_merge=256, tme=1024, tf=512, interpret=False):
    depth = w_in.shape[0]
    yp, ys = x_prompt, x_sample
    outs = [[] for _ in range(6)]
    for l in range(depth):
        yp, ys, *st = _layer(
            yp, ys, cache_k[l], cache_v[l], state_pool[l], g_norm1[l], w_in[l], g_q[l], g_k[l],
            attn_sinks[l], w_pool[l], pool_scale[l], w_a[l], w_b[l], w_out[l], g_norm2[l],
            w_router[l], b_router[l], w1[l], b1[l], w2[l], b2[l],
            tm_p=tm_p, tm_merge=tm_merge, tme=tme, tf=tf, interpret=interpret)
        for acc, val in zip(outs, st):
            acc.append(val)
    return (yp, ys) + tuple(jnp.stack(o) for o in outs)


def kernel(x_prompt, x_sample, cache_k, cache_v, state_pool, g_norm1, w_in, g_q, g_k, attn_sinks,
           w_pool, pool_scale, w_a, w_b, w_out, g_norm2, w_router, b_router, w1, b1, w2, b2):
    return _forward(x_prompt, x_sample, cache_k, cache_v, state_pool, g_norm1, w_in, g_q, g_k,
                    attn_sinks, w_pool, pool_scale, w_a, w_b, w_out, g_norm2, w_router, b_router,
                    w1, b1, w2, b2)
```
